```python
import math
import jax, jax.numpy as jnp
from jax import lax
import numpy as np

D_MODEL = 1024
BATCH = 4
SEQ = 4096
DEPTH = 2

D_MIX = D_MODEL
D_HALF = D_MIX // 2
S5_GROUP = 16
S5_GROUPS = D_HALF // S5_GROUP
S5_STATE = 64
S5_DT_MIN = 0.001
S5_DT_MAX = 0.1
SGU_CHUNK = 128
SGU_HEADS = 4
SGU_HEAD_DIM = D_HALF // SGU_HEADS
POOL_WINDOWS = (2, 4, 8, 16)
POOL_GROUP_DIM = D_HALF // len(POOL_WINDOWS)
HGRN_HEAD_DIM = 128
HGRN_HEADS = D_HALF // HGRN_HEAD_DIM
HGRN_CHUNK = 64
D_FF = 2816
N_EVEN = (DEPTH + 1) // 2
N_ODD = DEPTH // 2
EVEN_IN = 3 * D_HALF
ODD_IN = 5 * D_HALF
EPS = 1e-6

kernel_name = 'hybrid_s5_sgu_pool_hgrn2_macaron'

F32 = jnp.float32


def rmsnorm(x, g):
    xf = x.astype(F32)
    y = xf * lax.rsqrt(jnp.mean(xf * xf, axis=-1, keepdims=True) + EPS)
    return (y * g.astype(F32)).astype(x.dtype)


def swiglu(h, wg, wu, wd):
    return (jax.nn.silu(h @ wg) * (h @ wu)) @ wd


def _complex_affine_combine(earlier, later):
    a1r, a1i, b1r, b1i = earlier
    a2r, a2i, b2r, b2i = later
    return (a2r * a1r - a2i * a1i,
            a2r * a1i + a2i * a1r,
            a2r * b1r - a2i * b1i + b2r,
            a2r * b1i + a2i * b1r + b2i)


def s5_mixer(u, lam_re, lam_im, log_dt, b_re, b_im, c_re, c_im, d_skip, w_glu):
    bsz, seq, _ = u.shape
    uf = u.astype(F32).reshape(bsz, seq, S5_GROUPS, S5_GROUP)
    lr = lam_re.astype(F32)
    li = lam_im.astype(F32)
    dt = jnp.exp(log_dt.astype(F32))[:, None]
    mag = jnp.exp(lr * dt)
    a_re = mag * jnp.cos(li * dt)
    a_im = mag * jnp.sin(li * dt)
    den = lr * lr + li * li
    coef_re = ((a_re - 1.0) * lr + a_im * li) / den
    coef_im = (a_im * lr - (a_re - 1.0) * li) / den
    br = b_re.astype(F32)
    bi = b_im.astype(F32)
    bbar_re = coef_re[..., None] * br - coef_im[..., None] * bi
    bbar_im = coef_re[..., None] * bi + coef_im[..., None] * br
    bu_re = jnp.einsum('blgc,gpc->blgp', uf, bbar_re)
    bu_im = jnp.einsum('blgc,gpc->blgp', uf, bbar_im)
    a_re_b = jnp.broadcast_to(a_re, bu_re.shape)
    a_im_b = jnp.broadcast_to(a_im, bu_im.shape)
    _, _, x_re, x_im = lax.associative_scan(
        _complex_affine_combine, (a_re_b, a_im_b, bu_re, bu_im), axis=1)
    y = (jnp.einsum('blgp,gcp->blgc', x_re, c_re.astype(F32))
         - jnp.einsum('blgp,gcp->blgc', x_im, c_im.astype(F32)))
    y = y.reshape(bsz, seq, D_HALF) + d_skip.astype(F32) * uf.reshape(bsz, seq, D_HALF)
    y = jax.nn.gelu(y)
    y = y * jax.nn.sigmoid(y @ w_glu.astype(F32))
    return y.astype(u.dtype)


def sgu_mixer(zu, zv, norm_g, w_s, b_s):
    bsz, seq, _ = zu.shape
    u = jax.nn.gelu(zu.astype(F32))
    v = jax.nn.gelu(zv.astype(F32))
    mu = jnp.mean(v, axis=-1, keepdims=True)
    var = jnp.mean(jnp.square(v - mu), axis=-1, keepdims=True)
    vn = (v - mu) * lax.rsqrt(var + EPS) * norm_g.astype(F32)
    nch = seq // SGU_CHUNK
    vc = vn.reshape(bsz, nch, SGU_CHUNK, SGU_HEADS, SGU_HEAD_DIM)
    mask = jnp.tril(jnp.ones((SGU_CHUNK, SGU_CHUNK), dtype=bool))
    w = jnp.where(mask, w_s.astype(F32), 0.0)
    s = jnp.einsum('hts,bnshc->bnthc', w, vc) + b_s.astype(F32).T[None, None, :, :, None]
    return (u * s.reshape(bsz, seq, D_HALF)).astype(zu.dtype)


def pool_mixer(z, w_pool, scale):
    bsz, seq, _ = z.shape
    zf = z.astype(F32)
    csum = jnp.pad(jnp.cumsum(zf, axis=1), ((0, 0), (1, 0), (0, 0)))
    pos = jnp.arange(1, seq + 1, dtype=F32)
    outs = []
    for gi, win in enumerate(POOL_WINDOWS):
        sl = slice(gi * POOL_GROUP_DIM, (gi + 1) * POOL_GROUP_DIM)
        cg = csum[:, :, sl]
        lagged = jnp.pad(cg, ((0, 0), (win, 0), (0, 0)))[:, :seq + 1]
        wsum = (cg - lagged)[:, 1:]
        mean = wsum / jnp.minimum(pos, float(win))[None, :, None]
        outs.append((mean - zf[:, :, sl]) @ w_pool[gi].astype(F32))
    y = jnp.concatenate(outs, axis=-1) * scale.astype(F32)
    return y.astype(z.dtype)


def hgrn2_mixer(zq, zf, zi, zg, lb, onorm_g):
    bsz, seq, _ = zq.shape
    lbf = lb.astype(F32)
    q = jax.nn.silu(zq.astype(F32))
    fgate = lbf + (1.0 - lbf) * jax.nn.sigmoid(zf.astype(F32))
    logf = jnp.log(fgate)
    k = 1.0 - fgate
    v = zi.astype(F32)
    nch = seq // HGRN_CHUNK

    def to_chunks(t):
        return t.reshape(bsz, nch, HGRN_CHUNK, HGRN_HEADS, HGRN_HEAD_DIM).transpose(1, 0, 3, 2, 4)

    mask = jnp.tril(jnp.ones((HGRN_CHUNK, HGRN_CHUNK), dtype=bool))

    def step(state, inp):
        qc, kc, vc, lc = inp
        b = jnp.cumsum(lc, axis=2)
        diff = b[:, :, :, None, :] - b[:, :, None, :, :]
        decay = jnp.exp(jnp.where(mask[:, :, None], diff, -jnp.inf))
        scores = jnp.einsum('bhtk,bhtsk,bhsk->bhts', qc, decay, kc)
        o = (jnp.einsum('bhts,bhsv->bhtv', scores, vc)
             + jnp.einsum('bhtk,bhkv->bhtv', qc * jnp.exp(b), state))
        b_last = b[:, :, -1:, :]
        new_state = (jnp.exp(b_last[:, :, 0, :])[..., None] * state
                     + jnp.einsum('bhsk,bhsv->bhkv', kc * jnp.exp(b_last - b), vc))
        return new_state, o

    s0 = jnp.zeros((bsz, HGRN_HEADS, HGRN_HEAD_DIM, HGRN_HEAD_DIM), F32)
    _, o = lax.scan(step, s0, (to_chunks(q), to_chunks(k), to_chunks(v), to_chunks(logf)))
    o = o.transpose(1, 0, 3, 2, 4).reshape(bsz, seq, HGRN_HEADS, HGRN_HEAD_DIM)
    o = o * lax.rsqrt(jnp.mean(o * o, axis=-1, keepdims=True) + EPS) * onorm_g.astype(F32)
    g = jax.nn.silu(zg.astype(F32)).reshape(bsz, seq, HGRN_HEADS, HGRN_HEAD_DIM)
    return (o * g).reshape(bsz, seq, D_HALF).astype(zq.dtype)


def setup_inputs(seed: int = 0) -> dict:
    key = jax.random.key(seed)
    ks = jax.random.split(key, 32)

    def nrm(k, shape, scale=1.0):
        return jax.random.normal(k, shape, F32) * scale

    x = nrm(ks[0], (BATCH, SEQ, D_MODEL))
    norm_g = 1.0 + nrm(ks[1], (DEPTH, 6, D_MODEL), 0.02)
    ffn_wg = nrm(ks[2], (DEPTH, 2, D_MODEL, D_FF), D_MODEL ** -0.5)
    ffn_wu = nrm(ks[3], (DEPTH, 2, D_MODEL, D_FF), D_MODEL ** -0.5)
    ffn_wd = nrm(ks[4], (DEPTH, 2, D_FF, D_MODEL), D_FF ** -0.5)
    even_w_in = nrm(ks[5], (N_EVEN, D_MODEL, EVEN_IN), D_MODEL ** -0.5)
    even_w_out = nrm(ks[6], (N_EVEN, D_MIX, D_MODEL), D_MIX ** -0.5)
    s5_lam_re = -0.5 + nrm(ks[7], (N_EVEN, S5_GROUPS, S5_STATE), 0.01)
    s5_lam_im = (math.pi * jnp.arange(S5_STATE, dtype=F32))[None, None, :] + nrm(ks[8], (N_EVEN, S5_GROUPS, S5_STATE), 0.01)
    s5_log_dt = jax.random.uniform(ks[9], (N_EVEN, S5_GROUPS), F32, minval=math.log(S5_DT_MIN), maxval=math.log(S5_DT_MAX))
    s5_b_re = nrm(ks[10], (N_EVEN, S5_GROUPS, S5_STATE, S5_GROUP), (2 * S5_GROUP) ** -0.5)
    s5_b_im = nrm(ks[11], (N_EVEN, S5_GROUPS, S5_STATE, S5_GROUP), (2 * S5_GROUP) ** -0.5)
    s5_c_re = nrm(ks[12], (N_EVEN, S5_GROUPS, S5_GROUP, S5_STATE), S5_STATE ** -0.5)
    s5_c_im = nrm(ks[13], (N_EVEN, S5_GROUPS, S5_GROUP, S5_STATE), S5_STATE ** -0.5)
    s5_d = nrm(ks[14], (N_EVEN, D_HALF))
    s5_w_glu = nrm(ks[15], (N_EVEN, D_HALF, D_HALF), D_HALF ** -0.5)
    sgu_norm_g = 1.0 + nrm(ks[16], (N_EVEN, D_HALF), 0.02)
    sgu_w = nrm(ks[17], (N_EVEN, SGU_HEADS, SGU_CHUNK, SGU_CHUNK), SGU_CHUNK ** -0.5)
    sgu_b = 1.0 + nrm(ks[18], (N_EVEN, SGU_HEADS, SGU_CHUNK), 0.02)
    odd_w_in = nrm(ks[19], (N_ODD, D_MODEL, ODD_IN), D_MODEL ** -0.5)
    odd_w_out = nrm(ks[20], (N_ODD, D_MIX, D_MODEL), D_MIX ** -0.5)
    pool_w = nrm(ks[21], (N_ODD, len(POOL_WINDOWS), POOL_GROUP_DIM, POOL_GROUP_DIM), POOL_GROUP_DIM ** -0.5)
    pool_scale = 1.0 + nrm(ks[22], (N_ODD, D_HALF), 0.1)
    hgrn_lb = nrm(ks[23], (DEPTH, D_HALF))
    hgrn_onorm_g = 1.0 + nrm(ks[24], (N_ODD, HGRN_HEAD_DIM), 0.02)
    return {'x': x, 'norm_g': norm_g, 'ffn_wg': ffn_wg, 'ffn_wu': ffn_wu, 'ffn_wd': ffn_wd,
            'even_w_in': even_w_in, 'even_w_out': even_w_out,
            's5_lam_re': s5_lam_re, 's5_lam_im': s5_lam_im, 's5_log_dt': s5_log_dt,
            's5_b_re': s5_b_re, 's5_b_im': s5_b_im, 's5_c_re': s5_c_re, 's5_c_im': s5_c_im,
            's5_d': s5_d, 's5_w_glu': s5_w_glu,
            'sgu_norm_g': sgu_norm_g, 'sgu_w': sgu_w, 'sgu_b': sgu_b,
            'odd_w_in': odd_w_in, 'odd_w_out': odd_w_out,
            'pool_w': pool_w, 'pool_scale': pool_scale,
            'hgrn_lb': hgrn_lb, 'hgrn_onorm_g': hgrn_onorm_g}


def reference(x, norm_g, ffn_wg, ffn_wu, ffn_wd, even_w_in, even_w_out,
              s5_lam_re, s5_lam_im, s5_log_dt, s5_b_re, s5_b_im, s5_c_re, s5_c_im,
              s5_d, s5_w_glu, sgu_norm_g, sgu_w, sgu_b, odd_w_in, odd_w_out,
              pool_w, pool_scale, hgrn_lb, hgrn_onorm_g):
    sm = jax.nn.softmax(hgrn_lb.astype(F32), axis=0)
    lb_all = jnp.cumsum(sm, axis=0) - sm[0:1]
    for li in range(DEPTH):
        g = norm_g[li]
        j = li // 2
        h = rmsnorm(x, g[0])
        x = x + 0.5 * rmsnorm(swiglu(h, ffn_wg[li, 0], ffn_wu[li, 0], ffn_wd[li, 0]), g[1])
        h = rmsnorm(x, g[2])
        if li % 2 == 0:
            z = h @ even_w_in[j]
            ya = s5_mixer(z[..., :D_HALF], s5_lam_re[j], s5_lam_im[j], s5_log_dt[j],
                          s5_b_re[j], s5_b_im[j], s5_c_re[j], s5_c_im[j], s5_d[j], s5_w_glu[j])
            yb = sgu_mixer(z[..., D_HALF:2 * D_HALF], z[..., 2 * D_HALF:],
                           sgu_norm_g[j], sgu_w[j], sgu_b[j])
            y = jnp.concatenate([ya, yb], axis=-1) @ even_w_out[j]
        else:
            z = h @ odd_w_in[j]
            yc = pool_mixer(z[..., :D_HALF], pool_w[j], pool_scale[j])
            yd = hgrn2_mixer(z[..., D_HALF:2 * D_HALF], z[..., 2 * D_HALF:3 * D_HALF],
                             z[..., 3 * D_HALF:4 * D_HALF], z[..., 4 * D_HALF:],
                             lb_all[li], hgrn_onorm_g[j])
            y = jnp.concatenate([yc, yd], axis=-1) @ odd_w_out[j]
        x = x + rmsnorm(y, g[3])
        h = rmsnorm(x, g[4])
        x = x + 0.5 * rmsnorm(swiglu(h, ffn_wg[li, 1], ffn_wu[li, 1], ffn_wd[li, 1]), g[5])
    return x
```

```python
import functools
import math

import jax
import jax.numpy as jnp
from jax import lax
from jax.experimental import pallas as pl
from jax.experimental.pallas import tpu as pltpu

F32 = jnp.float32
BF16 = jnp.bfloat16
EPS = 1e-6

S5_GROUP = 16
S5_STATE = 64
SGU_CHUNK = 128
SGU_HEADS = 4
POOL_WINDOWS = (2, 4, 8, 16)
POOL_HALO = 128
HGRN_HEAD_DIM = 128

FFN_TOKENS = 512
MIX_CHUNK = 256
HGRN_LEVELS = int(math.log2(MIX_CHUNK))
VMEM_LIMIT = 56 * 1024 * 1024


def _rms(x, g):
    return x * lax.rsqrt(jnp.mean(x * x, axis=-1, keepdims=True) + EPS) * g


def _dot(a, b):
    return jnp.dot(a, b, preferred_element_type=F32)


def _dot_nt(a, b):
    return lax.dot_general(a, b, (((1,), (1,)), ((), ())), preferred_element_type=F32)


def _dot_tn(a, b):
    return lax.dot_general(a, b, (((0,), (0,)), ((), ())), preferred_element_type=F32)


def _const_spec(shape):
    nd = len(shape)
    return pl.BlockSpec(shape, lambda *_: (0,) * nd, pipeline_mode=pl.Buffered(1))


def _ffn_kernel(x_ref, gin_ref, gout_ref, wg_ref, wu_ref, wd_ref, o_ref):
    x = x_ref[...]
    h = _rms(x, gin_ref[...]).astype(BF16)
    a = _dot(h, wg_ref[...])
    b = _dot(h, wu_ref[...])
    p = (jax.nn.silu(a) * b).astype(BF16)
    y = _dot(p, wd_ref[...])
    o_ref[...] = x + 0.5 * _rms(y, gout_ref[...])


def _ffn(x2, g_in, g_out, wg, wu, wd):
    n, d = x2.shape
    dff = wg.shape[1]
    tm = min(FFN_TOKENS, n)
    return pl.pallas_call(
        _ffn_kernel,
        grid=(n // tm,),
        in_specs=[
            pl.BlockSpec((tm, d), lambda i: (i, 0)),
            _const_spec((1, d)), _const_spec((1, d)),
            _const_spec((d, dff)), _const_spec((d, dff)), _const_spec((dff, d)),
        ],
        out_specs=pl.BlockSpec((tm, d), lambda i: (i, 0)),
        out_shape=jax.ShapeDtypeStruct((n, d), F32),
        compiler_params=pltpu.CompilerParams(
            dimension_semantics=("arbitrary",), vmem_limit_bytes=VMEM_LIMIT),
        name="ffn",
    )(x2, g_in.reshape(1, d), g_out.reshape(1, d), wg.astype(BF16), wu.astype(BF16), wd.astype(BF16))


def _even_kernel(x_ref, gin_ref, gout_ref, win_ref, bbd_ref, cbd_ref, npre_ref, npim_ref,
                 ppre_ref, ppim_ref, are_ref, aim_ref, tri_ref, dskip_ref, wglu_ref,
                 sgug_ref, sguw_ref, sgub_ref, wout_ref, o_ref, cre_ref, cim_ref):
    t = x_ref.shape[1]
    dh = dskip_ref.shape[1]
    ns = npre_ref.shape[1]
    nh = ns // 2

    @pl.when(pl.program_id(1) == 0)
    def _():
        cre_ref[...] = jnp.zeros_like(cre_ref)
        cim_ref[...] = jnp.zeros_like(cim_ref)

    x = x_ref[0]
    h = _rms(x, gin_ref[...]).astype(BF16)
    z = _dot(h, win_ref[...])
    u = z[:, :dh]
    ub = u.astype(BF16)
    tri = tri_ref[...]

    ys = []
    for hf in range(2):
        sl = slice(hf * nh, (hf + 1) * nh)
        bu = _dot(ub[:, hf * (dh // 2):(hf + 1) * (dh // 2)], bbd_ref[hf])
        bre, bim = bu[:, :nh], bu[:, nh:]
        nr, ni = npre_ref[:, sl], npim_ref[:, sl]
        zre = (nr * bre - ni * bim).astype(BF16)
        zim = (nr * bim + ni * bre).astype(BF16)
        acc = _dot(tri, jnp.concatenate([zre, zim], axis=1))
        car, cai = cre_ref[:, sl], cim_ref[:, sl]
        ar, ai = are_ref[:, sl], aim_ref[:, sl]
        sre = acc[:, :nh] + (ar * car - ai * cai)
        sim = acc[:, nh:] + (ar * cai + ai * car)
        pr, pi = ppre_ref[:, sl], ppim_ref[:, sl]
        xre = pr * sre - pi * sim
        xim = pr * sim + pi * sre
        cre_ref[:, sl] = xre[t - 1:t, :]
        cim_ref[:, sl] = xim[t - 1:t, :]
        xs = jnp.concatenate([xre.astype(BF16), xim.astype(BF16)], axis=1)
        ys.append(_dot(xs, cbd_ref[hf]))
    y = jnp.concatenate(ys, axis=1) + dskip_ref[...] * u
    y = jax.nn.gelu(y)
    ya = y * jax.nn.sigmoid(_dot(y.astype(BF16), wglu_ref[...]))

    gu = jax.nn.gelu(z[:, dh:2 * dh])
    gv = jax.nn.gelu(z[:, 2 * dh:])
    mu = jnp.mean(gv, axis=-1, keepdims=True)
    dv = gv - mu
    var = jnp.mean(dv * dv, axis=-1, keepdims=True)
    vn = (dv * lax.rsqrt(var + EPS) * sgug_ref[...]).astype(BF16)
    hd = dh // SGU_HEADS
    rows = []
    for c in range(t // SGU_CHUNK):
        cols = []
        for hh in range(SGU_HEADS):
            blk = vn[c * SGU_CHUNK:(c + 1) * SGU_CHUNK, hh * hd:(hh + 1) * hd]
            cols.append(_dot(sguw_ref[hh], blk))
        rows.append(jnp.concatenate(cols, axis=1) + sgub_ref[...])
    s = jnp.concatenate(rows, axis=0) if len(rows) > 1 else rows[0]
    yb = gu * s

    ycat = jnp.concatenate([ya.astype(BF16), yb.astype(BF16)], axis=1)
    out = _dot(ycat, wout_ref[...])
    o_ref[0] = x + _rms(out, gout_ref[...])


def _s5_tables(lam_re, lam_im, log_dt, b_re, b_im, c_re, c_im, t):
    g, p = lam_re.shape
    gh = g // 2
    lr = lam_re.astype(F32)
    li = lam_im.astype(F32)
    dt = jnp.exp(log_dt.astype(F32))[:, None]
    mag = jnp.exp(lr * dt)
    a_re = mag * jnp.cos(li * dt)
    a_im = mag * jnp.sin(li * dt)
    den = lr * lr + li * li
    coef_re = ((a_re - 1.0) * lr + a_im * li) / den
    coef_im = (a_im * lr - (a_re - 1.0) * li) / den
    br = b_re.astype(F32)
    bi = b_im.astype(F32)
    bbar_re = coef_re[..., None] * br - coef_im[..., None] * bi
    bbar_im = coef_re[..., None] * bi + coef_im[..., None] * br
    eye = jnp.eye(gh, dtype=F32)

    def bd_in(w):
        c = w.shape[2]
        return jnp.einsum('gpc,gh->gchp', w, eye).reshape(gh * c, gh * p)

    def bd_out(w):
        c = w.shape[1]
        return jnp.einsum('gcp,gh->gphc', w, eye).reshape(gh * p, gh * c)

    bbd = jnp.stack([jnp.concatenate([bd_in(bbar_re[i * gh:(i + 1) * gh]),
                                      bd_in(bbar_im[i * gh:(i + 1) * gh])], axis=1) for i in range(2)])
    cbd = jnp.stack([jnp.concatenate([bd_out(c_re.astype(F32)[i * gh:(i + 1) * gh]),
                                      -bd_out(c_im.astype(F32)[i * gh:(i + 1) * gh])], axis=0) for i in range(2)])
    s = jnp.arange(t, dtype=F32)[:, None]
    ln_mag = (lr * dt).reshape(1, g * p)
    ang = (li * dt).reshape(1, g * p)
    pmag, nmag = jnp.exp(s * ln_mag), jnp.exp(-s * ln_mag)
    cs, sn = jnp.cos(s * ang), jnp.sin(s * ang)
    return dict(bbd=bbd.astype(BF16), cbd=cbd.astype(BF16),
                npre=nmag * cs, npim=-nmag * sn, ppre=pmag * cs, ppim=pmag * sn,
                are=a_re.reshape(1, g * p), aim=a_im.reshape(1, g * p))


def _even_layer(x, g_in, g_out, w_in, w_out, lam_re, lam_im, log_dt, b_re, b_im, c_re, c_im,
                d_skip, w_glu, sgu_g, sgu_w, sgu_b):
    bsz, seq, d = x.shape
    dh = d_skip.shape[0]
    t = min(MIX_CHUNK, seq)
    tb = _s5_tables(lam_re, lam_im, log_dt, b_re, b_im, c_re, c_im, t)
    ns = tb['npre'].shape[1]
    tri = jnp.tril(jnp.ones((t, t), F32)).astype(BF16)
    mask = jnp.tril(jnp.ones((SGU_CHUNK, SGU_CHUNK), dtype=bool))
    sguw = jnp.where(mask, sgu_w.astype(F32), 0.0).astype(BF16)
    hd = dh // SGU_HEADS
    sgub = jnp.repeat(sgu_b.astype(F32).T, hd, axis=1)
    xspec = pl.BlockSpec((1, t, d), lambda b, c: (b, c, 0))
    args = [x, g_in.reshape(1, d), g_out.reshape(1, d), w_in.astype(BF16), tb['bbd'], tb['cbd'],
            tb['npre'], tb['npim'], tb['ppre'], tb['ppim'], tb['are'], tb['aim'], tri,
            d_skip.reshape(1, dh).astype(F32), w_glu.astype(BF16), sgu_g.reshape(1, dh).astype(F32),
            sguw, sgub, w_out.astype(BF16)]
    return pl.pallas_call(
        _even_kernel,
        grid=(bsz, seq // t),
        in_specs=[xspec] + [_const_spec(a.shape) for a in args[1:]],
        out_specs=xspec,
        out_shape=jax.ShapeDtypeStruct(x.shape, F32),
        scratch_shapes=[pltpu.VMEM((1, ns), F32), pltpu.VMEM((1, ns), F32)],
        compiler_params=pltpu.CompilerParams(
            dimension_semantics=("arbitrary", "arbitrary"), vmem_limit_bytes=VMEM_LIMIT),
        name="even_mixer",
    )(*args)


def _odd_kernel(x_ref, gin_ref, gout_ref, win_ref, pm_ref, poolw_ref, pscale_ref, lb_ref,
                dall_ref, lvl_ref, onorm_ref, wout_ref, o_ref, halo_ref, st_ref):
    t = x_ref.shape[1]
    dh = lb_ref.shape[1]
    hd = HGRN_HEAD_DIM
    nheads = dh // hd
    nlev = dall_ref.shape[0] // t - 1

    @pl.when(pl.program_id(1) == 0)
    def _():
        halo_ref[...] = jnp.zeros_like(halo_ref)
        st_ref[...] = jnp.zeros_like(st_ref)

    x = x_ref[0]
    h = _rms(x, gin_ref[...]).astype(BF16)
    z = _dot(h, win_ref[...])

    zc = z[:, :dh].astype(BF16)
    zext = jnp.concatenate([halo_ref[...].astype(BF16), zc], axis=0)
    halo_ref[...] = zc[t - POOL_HALO:, :].astype(F32)
    gd = dh // len(POOL_WINDOWS)
    pooled = []
    later = jnp.minimum(pl.program_id(1), 1)
    for gi in range(len(POOL_WINDOWS)):
        m = _dot(pm_ref[later, gi], zext[:, gi * gd:(gi + 1) * gd])
        pooled.append(_dot(m.astype(BF16), poolw_ref[gi]))
    yc = jnp.concatenate(pooled, axis=1) * pscale_ref[...]

    lb = lb_ref[...]
    q = jax.nn.silu(z[:, dh:2 * dh])
    fg = lb + (1.0 - lb) * jax.nn.sigmoid(z[:, 2 * dh:3 * dh])
    logf = jnp.log(fg)
    k = 1.0 - fg
    v = z[:, 3 * dh:4 * dh].astype(BF16)
    hi = logf.astype(BF16)
    lo = (logf - hi.astype(F32)).astype(BF16)
    dall = dall_ref[...]
    gall = _dot(dall, hi) + _dot(dall, lo)
    lvl = lvl_ref[...]

    scores = [_dot_nt(q[:, i * hd:(i + 1) * hd].astype(BF16), k[:, i * hd:(i + 1) * hd].astype(BF16))
              for i in range(nheads)]
    scores = [jnp.where(lvl == nlev, s, 0.0) for s in scores]
    for lev in range(nlev):
        e = jnp.exp(-jnp.abs(gall[lev * t:(lev + 1) * t, :]))
        qe = (q * e).astype(BF16)
        ke = (k * e).astype(BF16)
        sel = lvl == lev
        for i in range(nheads):
            w = _dot_nt(qe[:, i * hd:(i + 1) * hd], ke[:, i * hd:(i + 1) * hd])
            scores[i] = jnp.where(sel, w, scores[i])

    b = gall[nlev * t:, :]
    blast = b[t - 1:t, :]
    qd = (q * jnp.exp(b)).astype(BF16)
    kd = (k * jnp.exp(blast - b)).astype(BF16)
    dec = jnp.exp(blast)
    outs = []
    for i in range(nheads):
        sl = slice(i * hd, (i + 1) * hd)
        st = st_ref[i]
        o = _dot(scores[i].astype(BF16), v[:, sl]) + _dot_nt(qd[:, sl], st.astype(BF16))
        st_ref[i] = st * dec[:, sl] + _dot_tn(v[:, sl], kd[:, sl])
        o = o * lax.rsqrt(jnp.mean(o * o, axis=-1, keepdims=True) + EPS) * onorm_ref[...]
        outs.append(o)
    yd = jnp.concatenate(outs, axis=1) * jax.nn.silu(z[:, 4 * dh:])

    ycat = jnp.concatenate([yc.astype(BF16), yd.astype(BF16)], axis=1)
    out = _dot(ycat, wout_ref[...])
    o_ref[0] = x + _rms(out, gout_ref[...])


def _pool_matrices(t):
    r = jnp.arange(t)[:, None]
    c = jnp.arange(t + POOL_HALO)[None, :] - POOL_HALO
    mats = []
    for first in (True, False):
        per = []
        for win in POOL_WINDOWS:
            band = (c <= r) & (c > r - win)
            if first:
                band = band & (c >= 0)
                cnt = jnp.minimum(r + 1, win).astype(F32)
            else:
                cnt = jnp.full((t, 1), float(win), F32)
            per.append(jnp.where(band, 1.0 / cnt, 0.0) - (c == r).astype(F32))
        mats.append(jnp.stack(per))
    return jnp.stack(mats).astype(BF16)


def _hgrn_level_matrices(t):
    nlev = int(math.log2(t))
    r = jnp.arange(t)[:, None]
    c = jnp.arange(t)[None, :]
    mats = []
    for lev in range(nlev):
        n = 1 << lev
        e = ((r >> (lev + 1)) << (lev + 1)) + n - 1
        upper = (c > e) & (c <= r)
        lower = (c > r) & (c <= e)
        mats.append(upper.astype(F32) - lower.astype(F32))
    mats.append((c <= r).astype(F32))
    x = r ^ c
    lvl = jnp.zeros((t, t), jnp.int32)
    for lev in range(1, nlev):
        lvl = lvl + (x >= (1 << lev)).astype(jnp.int32)
    lvl = jnp.where(r > c, lvl, jnp.where(r == c, nlev, -1))
    return jnp.concatenate(mats, axis=0).astype(BF16), lvl


def _odd_layer(x, g_in, g_out, w_in, w_out, pool_w, pool_scale, lb, onorm_g):
    bsz, seq, d = x.shape
    dh = lb.shape[0]
    t = min(MIX_CHUNK, seq)
    pm = _pool_matrices(t)
    dall, lvl = _hgrn_level_matrices(t)
    nheads = dh // HGRN_HEAD_DIM
    xspec = pl.BlockSpec((1, t, d), lambda b, c: (b, c, 0))
    args = [x, g_in.reshape(1, d), g_out.reshape(1, d), w_in.astype(BF16), pm, pool_w.astype(BF16),
            pool_scale.reshape(1, dh).astype(F32), lb.reshape(1, dh).astype(F32), dall, lvl,
            onorm_g.reshape(1, HGRN_HEAD_DIM).astype(F32), w_out.astype(BF16)]
    return pl.pallas_call(
        _odd_kernel,
        grid=(bsz, seq // t),
        in_specs=[xspec] + [_const_spec(a.shape) for a in args[1:]],
        out_specs=xspec,
        out_shape=jax.ShapeDtypeStruct(x.shape, F32),
        scratch_shapes=[pltpu.VMEM((POOL_HALO, dh), F32),
                        pltpu.VMEM((nheads, HGRN_HEAD_DIM, HGRN_HEAD_DIM), F32)],
        compiler_params=pltpu.CompilerParams(
            dimension_semantics=("arbitrary", "arbitrary"), vmem_limit_bytes=VMEM_LIMIT),
        name="odd_mixer",
    )(*args)


def kernel(x, norm_g, ffn_wg, ffn_wu, ffn_wd, even_w_in, even_w_out, s5_lam_re, s5_lam_im, s5_log_dt, s5_b_re, s5_b_im, s5_c_re, s5_c_im, s5_d, s5_w_glu, sgu_norm_g, sgu_w, sgu_b, odd_w_in, odd_w_out, pool_w, pool_scale, hgrn_lb, hgrn_onorm_g):
    bsz, seq, d = x.shape
    depth = norm_g.shape[0]
    sm = jax.nn.softmax(hgrn_lb.astype(F32), axis=0)
    lb_all = jnp.cumsum(sm, axis=0) - sm[0:1]

    def ffn(xx, li, half, g_in, g_out):
        y = _ffn(xx.reshape(bsz * seq, d), g_in, g_out, ffn_wg[li, half], ffn_wu[li, half], ffn_wd[li, half])
        return y.reshape(bsz, seq, d)

    for li in range(depth):
        g = norm_g[li].astype(F32)
        j = li // 2
        x = ffn(x, li, 0, g[0], g[1])
        if li % 2 == 0:
            x = _even_layer(x, g[2], g[3], even_w_in[j], even_w_out[j], s5_lam_re[j], s5_lam_im[j],
                            s5_log_dt[j], s5_b_re[j], s5_b_im[j], s5_c_re[j], s5_c_im[j], s5_d[j],
                            s5_w_glu[j], sgu_norm_g[j], sgu_w[j], sgu_b[j])
        else:
            x = _odd_layer(x, g[2], g[3], odd_w_in[j], odd_w_out[j], pool_w[j], pool_scale[j],
                           lb_all[li], hgrn_onorm_g[j])
        x = ffn(x, li, 1, g[4], g[5])
    return x
```

```python
import functools
import math

import jax
import jax.numpy as jnp
from jax import lax
from jax.experimental import pallas as pl
from jax.experimental.pallas import tpu as pltpu

F32 = jnp.float32
BF16 = jnp.bfloat16
EPS = 1e-6

S5_GROUP = 16
S5_STATE = 64
SGU_CHUNK = 128
SGU_HEADS = 4
POOL_WINDOWS = (2, 4, 8, 16)
POOL_HALO = 128
HGRN_HEAD_DIM = 128

FFN_TOKENS = 512
MIX_CHUNK = 256
VMEM_LIMIT = 56 * 1024 * 1024


def _rms(x, g):
    return x * lax.rsqrt(jnp.mean(x * x, axis=-1, keepdims=True) + EPS) * g


def _dot(a, b):
    return jnp.dot(a, b, preferred_element_type=F32)


def _dot_nt(a, b):
    return lax.dot_general(a, b, (((1,), (1,)), ((), ())), preferred_element_type=F32)


def _dot_tn(a, b):
    return lax.dot_general(a, b, (((0,), (0,)), ((), ())), preferred_element_type=F32)


def _const_spec(shape):
    nd = len(shape)
    return pl.BlockSpec(shape, lambda *_: (0,) * nd, pipeline_mode=pl.Buffered(1))


def _ffn_kernel(x_ref, gin_ref, gout_ref, wg_ref, wu_ref, wd_ref, o_ref):
    x = x_ref[...]
    h = _rms(x, gin_ref[...]).astype(BF16)
    a = _dot(h, wg_ref[...])
    b = _dot(h, wu_ref[...])
    p = (jax.nn.silu(a) * b).astype(BF16)
    y = _dot(p, wd_ref[...])
    o_ref[...] = x + 0.5 * _rms(y, gout_ref[...])


def _ffn(x2, g_in, g_out, wg, wu, wd, li, half):
    n, d = x2.shape
    dff = wg.shape[-1]
    tm = min(FFN_TOKENS, n)

    def wspec(rows, cols):
        return pl.BlockSpec((None, None, rows, cols), lambda i: (li, half, 0, 0), pipeline_mode=pl.Buffered(1))

    return pl.pallas_call(
        _ffn_kernel,
        grid=(n // tm,),
        in_specs=[
            pl.BlockSpec((tm, d), lambda i: (i, 0)),
            _const_spec((1, d)), _const_spec((1, d)),
            wspec(d, dff), wspec(d, dff), wspec(dff, d),
        ],
        out_specs=pl.BlockSpec((tm, d), lambda i: (i, 0)),
        out_shape=jax.ShapeDtypeStruct((n, d), F32),
        compiler_params=pltpu.CompilerParams(
            dimension_semantics=("arbitrary",), vmem_limit_bytes=VMEM_LIMIT),
        name="ffn",
    )(x2, g_in.reshape(1, d), g_out.reshape(1, d), wg, wu, wd)


def _even_kernel(x_ref, gin_ref, gout_ref, win_ref, bbd_ref, cbd_ref, npre_ref, npim_ref,
                 ppre_ref, ppim_ref, are_ref, aim_ref, tri_ref, dskip_ref, wglu_ref,
                 sgug_ref, sguw_ref, sgub_ref, wout_ref, o_ref, cre_ref, cim_ref):
    t = x_ref.shape[1]
    dh = dskip_ref.shape[1]
    ns = npre_ref.shape[1]
    nh = ns // 2

    @pl.when(pl.program_id(1) == 0)
    def _():
        cre_ref[...] = jnp.zeros_like(cre_ref)
        cim_ref[...] = jnp.zeros_like(cim_ref)

    x = x_ref[0]
    h = _rms(x, gin_ref[...]).astype(BF16)
    z = _dot(h, win_ref[...])
    u = z[:, :dh]
    ub = u.astype(BF16)
    tri = tri_ref[...]

    ys = []
    for hf in range(2):
        sl = slice(hf * nh, (hf + 1) * nh)
        bu = _dot(ub[:, hf * (dh // 2):(hf + 1) * (dh // 2)], bbd_ref[hf])
        bre, bim = bu[:, :nh], bu[:, nh:]
        nr, ni = npre_ref[:, sl], npim_ref[:, sl]
        zre = (nr * bre - ni * bim).astype(BF16)
        zim = (nr * bim + ni * bre).astype(BF16)
        acc = _dot(tri, jnp.concatenate([zre, zim], axis=1))
        car, cai = cre_ref[:, sl], cim_ref[:, sl]
        ar, ai = are_ref[:, sl], aim_ref[:, sl]
        sre = acc[:, :nh] + (ar * car - ai * cai)
        sim = acc[:, nh:] + (ar * cai + ai * car)
        pr, pi = ppre_ref[:, sl], ppim_ref[:, sl]
        xre = pr * sre - pi * sim
        xim = pr * sim + pi * sre
        cre_ref[:, sl] = xre[t - 1:t, :]
        cim_ref[:, sl] = xim[t - 1:t, :]
        xs = jnp.concatenate([xre.astype(BF16), xim.astype(BF16)], axis=1)
        ys.append(_dot(xs, cbd_ref[hf]))
    y = jnp.concatenate(ys, axis=1) + dskip_ref[...] * u
    y = jax.nn.gelu(y)
    ya = y * jax.nn.sigmoid(_dot(y.astype(BF16), wglu_ref[...]))

    gu = jax.nn.gelu(z[:, dh:2 * dh])
    gv = jax.nn.gelu(z[:, 2 * dh:])
    mu = jnp.mean(gv, axis=-1, keepdims=True)
    dv = gv - mu
    var = jnp.mean(dv * dv, axis=-1, keepdims=True)
    vn = (dv * lax.rsqrt(var + EPS) * sgug_ref[...]).astype(BF16)
    hd = dh // SGU_HEADS
    rows = []
    for c in range(t // SGU_CHUNK):
        cols = []
        for hh in range(SGU_HEADS):
            blk = vn[c * SGU_CHUNK:(c + 1) * SGU_CHUNK, hh * hd:(hh + 1) * hd]
            cols.append(_dot(sguw_ref[hh], blk))
        rows.append(jnp.concatenate(cols, axis=1) + sgub_ref[...])
    s = jnp.concatenate(rows, axis=0) if len(rows) > 1 else rows[0]
    yb = gu * s

    ycat = jnp.concatenate([ya.astype(BF16), yb.astype(BF16)], axis=1)
    out = _dot(ycat, wout_ref[...])
    o_ref[0] = x + _rms(out, gout_ref[...])


def _s5_tables(lam_re, lam_im, log_dt, b_re, b_im, c_re, c_im, t):
    g, p = lam_re.shape
    gh = g // 2
    lr = lam_re.astype(F32)
    li = lam_im.astype(F32)
    dt = jnp.exp(log_dt.astype(F32))[:, None]
    mag = jnp.exp(lr * dt)
    a_re = mag * jnp.cos(li * dt)
    a_im = mag * jnp.sin(li * dt)
    den = lr * lr + li * li
    coef_re = ((a_re - 1.0) * lr + a_im * li) / den
    coef_im = (a_im * lr - (a_re - 1.0) * li) / den
    br = b_re.astype(F32)
    bi = b_im.astype(F32)
    bbar_re = coef_re[..., None] * br - coef_im[..., None] * bi
    bbar_im = coef_re[..., None] * bi + coef_im[..., None] * br
    eye = jnp.eye(gh, dtype=F32)

    def bd_in(w):
        c = w.shape[2]
        return jnp.einsum('gpc,gh->gchp', w, eye).reshape(gh * c, gh * p)

    def bd_out(w):
        c = w.shape[1]
        return jnp.einsum('gcp,gh->gphc', w, eye).reshape(gh * p, gh * c)

    bbd = jnp.stack([jnp.concatenate([bd_in(bbar_re[i * gh:(i + 1) * gh]),
                                      bd_in(bbar_im[i * gh:(i + 1) * gh])], axis=1) for i in range(2)])
    cbd = jnp.stack([jnp.concatenate([bd_out(c_re.astype(F32)[i * gh:(i + 1) * gh]),
                                      -bd_out(c_im.astype(F32)[i * gh:(i + 1) * gh])], axis=0) for i in range(2)])
    s = jnp.arange(t, dtype=F32)[:, None]
    ln_mag = (lr * dt).reshape(1, g * p)
    ang = (li * dt).reshape(1, g * p)
    pmag, nmag = jnp.exp(s * ln_mag), jnp.exp(-s * ln_mag)
    cs, sn = jnp.cos(s * ang), jnp.sin(s * ang)
    return dict(bbd=bbd.astype(BF16), cbd=cbd.astype(BF16),
                npre=nmag * cs, npim=-nmag * sn, ppre=pmag * cs, ppim=pmag * sn,
                are=a_re.reshape(1, g * p), aim=a_im.reshape(1, g * p))


def _even_layer(x, g_in, g_out, w_in, w_out, lam_re, lam_im, log_dt, b_re, b_im, c_re, c_im,
                d_skip, w_glu, sgu_g, sgu_w, sgu_b):
    bsz, seq, d = x.shape
    dh = d_skip.shape[0]
    t = min(MIX_CHUNK, seq)
    tb = _s5_tables(lam_re, lam_im, log_dt, b_re, b_im, c_re, c_im, t)
    ns = tb['npre'].shape[1]
    tri = jnp.tril(jnp.ones((t, t), F32)).astype(BF16)
    mask = jnp.tril(jnp.ones((SGU_CHUNK, SGU_CHUNK), dtype=bool))
    sguw = jnp.where(mask, sgu_w.astype(F32), 0.0).astype(BF16)
    hd = dh // SGU_HEADS
    sgub = jnp.repeat(sgu_b.astype(F32).T, hd, axis=1)
    xspec = pl.BlockSpec((1, t, d), lambda b, c: (b, c, 0))
    args = [x, g_in.reshape(1, d), g_out.reshape(1, d), w_in.astype(BF16), tb['bbd'], tb['cbd'],
            tb['npre'], tb['npim'], tb['ppre'], tb['ppim'], tb['are'], tb['aim'], tri,
            d_skip.reshape(1, dh).astype(F32), w_glu.astype(BF16), sgu_g.reshape(1, dh).astype(F32),
            sguw, sgub, w_out.astype(BF16)]
    return pl.pallas_call(
        _even_kernel,
        grid=(bsz, seq // t),
        in_specs=[xspec] + [_const_spec(a.shape) for a in args[1:]],
        out_specs=xspec,
        out_shape=jax.ShapeDtypeStruct(x.shape, F32),
        scratch_shapes=[pltpu.VMEM((1, ns), F32), pltpu.VMEM((1, ns), F32)],
        compiler_params=pltpu.CompilerParams(
            dimension_semantics=("arbitrary", "arbitrary"), vmem_limit_bytes=VMEM_LIMIT),
        name="even_mixer",
    )(*args)


def _odd_kernel(x_ref, gin_ref, gout_ref, win_ref, pm_ref, poolw_ref, pscale_ref, lb_ref,
                dall_ref, lvl_ref, onorm_ref, wout_ref, o_ref, halo_ref, st_ref):
    t = x_ref.shape[1]
    dh = lb_ref.shape[1]
    hd = HGRN_HEAD_DIM
    nheads = dh // hd
    nlev = int(math.log2(t))

    @pl.when(pl.program_id(1) == 0)
    def _():
        halo_ref[...] = jnp.zeros_like(halo_ref)
        st_ref[...] = jnp.zeros_like(st_ref)

    x = x_ref[0]
    h = _rms(x, gin_ref[...]).astype(BF16)
    z = _dot(h, win_ref[...])

    zc = z[:, :dh].astype(BF16)
    zext = jnp.concatenate([halo_ref[...].astype(BF16), zc], axis=0)
    halo_ref[...] = zc[t - POOL_HALO:, :].astype(F32)
    gd = dh // len(POOL_WINDOWS)
    pooled = []
    later = jnp.minimum(pl.program_id(1), 1)
    for gi in range(len(POOL_WINDOWS)):
        m = _dot(pm_ref[later, gi], zext[:, gi * gd:(gi + 1) * gd])
        pooled.append(_dot(m.astype(BF16), poolw_ref[gi]))
    yc = jnp.concatenate(pooled, axis=1) * pscale_ref[...]

    lb = lb_ref[...]
    q = jax.nn.silu(z[:, dh:2 * dh])
    fg = lb + (1.0 - lb) * jax.nn.sigmoid(z[:, 2 * dh:3 * dh])
    logf = jnp.log(fg)
    k = 1.0 - fg
    v = z[:, 3 * dh:4 * dh].astype(BF16)
    hi = logf.astype(BF16)
    lo = (logf - hi.astype(F32)).astype(BF16)
    dall = dall_ref[...]
    gall = _dot(dall, hi) + _dot(dall, lo)
    b = gall[t:, :]
    lvl = lvl_ref[...]

    def level_decay(lev):
        if lev == 1:
            return jnp.exp(-jnp.abs(gall[:t, :]))
        n = 1 << lev
        parts = []
        for s0 in range(0, t, 2 * n):
            m = jnp.broadcast_to(b[s0 + n - 1:s0 + n, :], (2 * n, dh))
            parts.append(jnp.exp(-jnp.abs(b[s0:s0 + 2 * n, :] - m)))
        return jnp.concatenate(parts, axis=0) if len(parts) > 1 else parts[0]

    kb = k.astype(BF16)
    scores = [_dot_nt(q[:, i * hd:(i + 1) * hd].astype(BF16), kb[:, i * hd:(i + 1) * hd])
              for i in range(nheads)]
    scores = [jnp.where(lvl == nlev, s, 0.0) for s in scores]
    for lev in range(nlev):
        if lev == 0:
            qe, ke = (q * fg).astype(BF16), kb
        else:
            e = level_decay(lev)
            qe, ke = (q * e).astype(BF16), (k * e).astype(BF16)
        sel = lvl == lev
        for i in range(nheads):
            w = _dot_nt(qe[:, i * hd:(i + 1) * hd], ke[:, i * hd:(i + 1) * hd])
            scores[i] = jnp.where(sel, w, scores[i])

    blast = b[t - 1:t, :]
    qd = (q * jnp.exp(b)).astype(BF16)
    kd = (k * jnp.exp(blast - b)).astype(BF16)
    dec = jnp.exp(blast)
    outs = []
    for i in range(nheads):
        sl = slice(i * hd, (i + 1) * hd)
        st = st_ref[i]
        o = _dot(scores[i].astype(BF16), v[:, sl]) + _dot_nt(qd[:, sl], st.astype(BF16))
        st_ref[i] = st * dec[:, sl] + _dot_tn(v[:, sl], kd[:, sl])
        o = o * lax.rsqrt(jnp.mean(o * o, axis=-1, keepdims=True) + EPS) * onorm_ref[...]
        outs.append(o)
    yd = jnp.concatenate(outs, axis=1) * jax.nn.silu(z[:, 4 * dh:])

    ycat = jnp.concatenate([yc.astype(BF16), yd.astype(BF16)], axis=1)
    out = _dot(ycat, wout_ref[...])
    o_ref[0] = x + _rms(out, gout_ref[...])


def _pool_matrices(t):
    r = jnp.arange(t)[:, None]
    c = jnp.arange(t + POOL_HALO)[None, :] - POOL_HALO
    mats = []
    for first in (True, False):
        per = []
        for win in POOL_WINDOWS:
            band = (c <= r) & (c > r - win)
            if first:
                band = band & (c >= 0)
                cnt = jnp.minimum(r + 1, win).astype(F32)
            else:
                cnt = jnp.full((t, 1), float(win), F32)
            per.append(jnp.where(band, 1.0 / cnt, 0.0) - (c == r).astype(F32))
        mats.append(jnp.stack(per))
    return jnp.stack(mats).astype(BF16)


def _hgrn_level_matrices(t):
    nlev = int(math.log2(t))
    r = jnp.arange(t)[:, None]
    c = jnp.arange(t)[None, :]
    e = ((r >> 2) << 2) + 1
    upper = (c > e) & (c <= r)
    lower = (c > r) & (c <= e)
    mats = [upper.astype(F32) - lower.astype(F32), (c <= r).astype(F32)]
    x = r ^ c
    lvl = jnp.zeros((t, t), jnp.int32)
    for lev in range(1, nlev):
        lvl = lvl + (x >= (1 << lev)).astype(jnp.int32)
    lvl = jnp.where(r > c, lvl, jnp.where(r == c, nlev, -1))
    return jnp.concatenate(mats, axis=0).astype(BF16), lvl


def _odd_layer(x, g_in, g_out, w_in, w_out, pool_w, pool_scale, lb, onorm_g):
    bsz, seq, d = x.shape
    dh = lb.shape[0]
    t = min(MIX_CHUNK, seq)
    pm = _pool_matrices(t)
    dall, lvl = _hgrn_level_matrices(t)
    nheads = dh // HGRN_HEAD_DIM
    xspec = pl.BlockSpec((1, t, d), lambda b, c: (b, c, 0))
    args = [x, g_in.reshape(1, d), g_out.reshape(1, d), w_in.astype(BF16), pm, pool_w.astype(BF16),
            pool_scale.reshape(1, dh).astype(F32), lb.reshape(1, dh).astype(F32), dall, lvl,
            onorm_g.reshape(1, HGRN_HEAD_DIM).astype(F32), w_out.astype(BF16)]
    return pl.pallas_call(
        _odd_kernel,
        grid=(bsz, seq // t),
        in_specs=[xspec] + [_const_spec(a.shape) for a in args[1:]],
        out_specs=xspec,
        out_shape=jax.ShapeDtypeStruct(x.shape, F32),
        scratch_shapes=[pltpu.VMEM((POOL_HALO, dh), F32),
                        pltpu.VMEM((nheads, HGRN_HEAD_DIM, HGRN_HEAD_DIM), F32)],
        compiler_params=pltpu.CompilerParams(
            dimension_semantics=("arbitrary", "arbitrary"), vmem_limit_bytes=VMEM_LIMIT),
        name="odd_mixer",
    )(*args)


def kernel(x, norm_g, ffn_wg, ffn_wu, ffn_wd, even_w_in, even_w_out, s5_lam_re, s5_lam_im, s5_log_dt, s5_b_re, s5_b_im, s5_c_re, s5_c_im, s5_d, s5_w_glu, sgu_norm_g, sgu_w, sgu_b, odd_w_in, odd_w_out, pool_w, pool_scale, hgrn_lb, hgrn_onorm_g):
    bsz, seq, d = x.shape
    depth = norm_g.shape[0]
    sm = jax.nn.softmax(hgrn_lb.astype(F32), axis=0)
    lb_all = jnp.cumsum(sm, axis=0) - sm[0:1]

    wg, wu, wd = ffn_wg.astype(BF16), ffn_wu.astype(BF16), ffn_wd.astype(BF16)

    def ffn(xx, li, half, g_in, g_out):
        y = _ffn(xx.reshape(bsz * seq, d), g_in, g_out, wg, wu, wd, li, half)
        return y.reshape(bsz, seq, d)

    for li in range(depth):
        g = norm_g[li].astype(F32)
        j = li // 2
        x = ffn(x, li, 0, g[0], g[1])
        if li % 2 == 0:
            x = _even_layer(x, g[2], g[3], even_w_in[j], even_w_out[j], s5_lam_re[j], s5_lam_im[j],
                            s5_log_dt[j], s5_b_re[j], s5_b_im[j], s5_c_re[j], s5_c_im[j], s5_d[j],
                            s5_w_glu[j], sgu_norm_g[j], sgu_w[j], sgu_b[j])
        else:
            x = _odd_layer(x, g[2], g[3], odd_w_in[j], odd_w_out[j], pool_w[j], pool_scale[j],
                           lb_all[li], hgrn_onorm_g[j])
        x = ffn(x, li, 1, g[4], g[5])
    return x
```

```python
import functools
import math

import jax
import jax.numpy as jnp
from jax import lax
from jax.experimental import pallas as pl
from jax.experimental.pallas import tpu as pltpu

F32 = jnp.float32
BF16 = jnp.bfloat16
EPS = 1e-6

S5_GROUP = 16
S5_STATE = 64
S5_LANE_TILE = 128
SGU_CHUNK = 128
SGU_HEADS = 4
POOL_WINDOWS = (2, 4, 8, 16)
POOL_HALO = 128
HGRN_HEAD_DIM = 128
HGRN_BLOCK = 128

FFN_TOKENS = 512
MIX_CHUNK = 256
VMEM_LIMIT = 56 * 1024 * 1024


def _rms(x, g):
    return x * lax.rsqrt(jnp.mean(x * x, axis=-1, keepdims=True) + EPS) * g


def _dot(a, b):
    return jnp.dot(a, b, preferred_element_type=F32)


def _dot_nt(a, b):
    return lax.dot_general(a, b, (((1,), (1,)), ((), ())), preferred_element_type=F32)


def _dot_tn(a, b):
    return lax.dot_general(a, b, (((0,), (0,)), ((), ())), preferred_element_type=F32)


def _skewed(stages, n, fillers=()):
    vals = [None] * n
    fillers = list(fillers)
    for step in range(n + len(stages) - 1):
        for k, stage in enumerate(stages):
            i = step - k
            if 0 <= i < n:
                vals[i] = stage(i, vals[i])
        if fillers:
            fillers.pop(0)()
    for f in fillers:
        f()
    return vals


def _const_spec(shape):
    nd = len(shape)
    return pl.BlockSpec(shape, lambda *_: (0,) * nd, pipeline_mode=pl.Buffered(1))


def _ffn_kernel(x_ref, gin_ref, gout_ref, wg_ref, wu_ref, wd_ref, o_ref):
    x = x_ref[...]
    h = _rms(x, gin_ref[...]).astype(BF16)
    a = _dot(h, wg_ref[...])
    b = _dot(h, wu_ref[...])
    p = (jax.nn.silu(a) * b).astype(BF16)
    y = _dot(p, wd_ref[...])
    o_ref[...] = x + 0.5 * _rms(y, gout_ref[...])


def _ffn(x2, g_in, g_out, wg, wu, wd, li, half):
    n, d = x2.shape
    dff = wg.shape[-1]
    tm = min(FFN_TOKENS, n)

    def wspec(rows, cols):
        return pl.BlockSpec((None, None, rows, cols), lambda i: (li, half, 0, 0), pipeline_mode=pl.Buffered(1))

    return pl.pallas_call(
        _ffn_kernel,
        grid=(n // tm,),
        in_specs=[
            pl.BlockSpec((tm, d), lambda i: (i, 0)),
            _const_spec((1, d)), _const_spec((1, d)),
            wspec(d, dff), wspec(d, dff), wspec(dff, d),
        ],
        out_specs=pl.BlockSpec((tm, d), lambda i: (i, 0)),
        out_shape=jax.ShapeDtypeStruct((n, d), F32),
        compiler_params=pltpu.CompilerParams(
            dimension_semantics=("arbitrary",), vmem_limit_bytes=VMEM_LIMIT),
        name="ffn",
    )(x2, g_in.reshape(1, d), g_out.reshape(1, d), wg, wu, wd)


def _even_kernel(x_ref, gin_ref, gout_ref, win_ref, bbd_ref, cbd_ref, npre_ref, npim_ref,
                 ppre_ref, ppim_ref, are_ref, aim_ref, tri_ref, dskip_ref, wglu_ref,
                 sgug_ref, sguw_ref, sgub_ref, wout_ref, o_ref, cre_ref, cim_ref):
    t = x_ref.shape[1]
    dh = dskip_ref.shape[1]
    ns = npre_ref.shape[1]
    nh = ns // 2

    @pl.when(pl.program_id(1) == 0)
    def _():
        cre_ref[...] = jnp.zeros_like(cre_ref)
        cim_ref[...] = jnp.zeros_like(cim_ref)

    x = x_ref[0]
    h = _rms(x, gin_ref[...]).astype(BF16)
    z = _dot(h, win_ref[...])
    u = z[:, :dh]
    ub = u.astype(BF16)
    tri = tri_ref[...]

    lt = S5_LANE_TILE
    tph = nh // lt
    ys = [None, None]

    def tile_slices(i):
        hf, j = divmod(i, tph)
        return hf, slice(hf * nh + j * lt, hf * nh + (j + 1) * lt), slice(2 * j * lt, 2 * (j + 1) * lt)

    def s5_input(i, _):
        hf, _, ws = tile_slices(i)
        return _dot(ub[:, hf * (dh // 2):(hf + 1) * (dh // 2)], bbd_ref[hf, :, ws])

    def s5_prescale(i, bu):
        _, sl, _ = tile_slices(i)
        bre, bim = bu[:, :lt], bu[:, lt:]
        nr, ni = npre_ref[:, sl], npim_ref[:, sl]
        return jnp.concatenate([(nr * bre - ni * bim).astype(BF16),
                                (nr * bim + ni * bre).astype(BF16)], axis=1)

    def s5_prefix(i, zs):
        return _dot(tri, zs)

    def s5_state(i, acc):
        _, sl, _ = tile_slices(i)
        car, cai = cre_ref[:, sl], cim_ref[:, sl]
        ar, ai = are_ref[:, sl], aim_ref[:, sl]
        sre = acc[:, :lt] + (ar * car - ai * cai)
        sim = acc[:, lt:] + (ar * cai + ai * car)
        pr, pi = ppre_ref[:, sl], ppim_ref[:, sl]
        xre = pr * sre - pi * sim
        xim = pr * sim + pi * sre
        cre_ref[:, sl] = xre[t - 1:t, :]
        cim_ref[:, sl] = xim[t - 1:t, :]
        return jnp.concatenate([xre.astype(BF16), xim.astype(BF16)], axis=1)

    def s5_output(i, xs):
        hf, _, ws = tile_slices(i)
        part = _dot(xs, cbd_ref[hf, ws, :])
        ys[hf] = part if ys[hf] is None else ys[hf] + part

    _skewed([s5_input, s5_prescale, s5_prefix, s5_state, s5_output], 2 * tph)
    y = jnp.concatenate(ys, axis=1) + dskip_ref[...] * u
    y = jax.nn.gelu(y)
    ya = y * jax.nn.sigmoid(_dot(y.astype(BF16), wglu_ref[...]))

    gu = jax.nn.gelu(z[:, dh:2 * dh])
    gv = jax.nn.gelu(z[:, 2 * dh:])
    mu = jnp.mean(gv, axis=-1, keepdims=True)
    dv = gv - mu
    var = jnp.mean(dv * dv, axis=-1, keepdims=True)
    vn = (dv * lax.rsqrt(var + EPS) * sgug_ref[...]).astype(BF16)
    hd = dh // SGU_HEADS
    rows = []
    for c in range(t // SGU_CHUNK):
        cols = [_dot(sguw_ref[hh], vn[c * SGU_CHUNK:(c + 1) * SGU_CHUNK, hh * hd:(hh + 1) * hd])
                for hh in range(SGU_HEADS)]
        rows.append(jnp.concatenate(cols, axis=1) + sgub_ref[...])
    s = jnp.concatenate(rows, axis=0) if len(rows) > 1 else rows[0]
    yb = gu * s

    ycat = jnp.concatenate([ya.astype(BF16), yb.astype(BF16)], axis=1)
    out = _dot(ycat, wout_ref[...])
    o_ref[0] = x + _rms(out, gout_ref[...])


def _s5_tables(lam_re, lam_im, log_dt, b_re, b_im, c_re, c_im, t):
    g, p = lam_re.shape
    gh = g // 2
    lr = lam_re.astype(F32)
    li = lam_im.astype(F32)
    dt = jnp.exp(log_dt.astype(F32))[:, None]
    mag = jnp.exp(lr * dt)
    a_re = mag * jnp.cos(li * dt)
    a_im = mag * jnp.sin(li * dt)
    den = lr * lr + li * li
    coef_re = ((a_re - 1.0) * lr + a_im * li) / den
    coef_im = (a_im * lr - (a_re - 1.0) * li) / den
    br = b_re.astype(F32)
    bi = b_im.astype(F32)
    bbar_re = coef_re[..., None] * br - coef_im[..., None] * bi
    bbar_im = coef_re[..., None] * bi + coef_im[..., None] * br
    eye = jnp.eye(gh, dtype=F32)

    def bd_in(w):
        c = w.shape[2]
        return jnp.einsum('gpc,gh->gchp', w, eye).reshape(gh * c, gh * p)

    def bd_out(w):
        c = w.shape[1]
        return jnp.einsum('gcp,gh->gphc', w, eye).reshape(gh * p, gh * c)

    lt = S5_LANE_TILE
    nt = gh * p // lt

    def tile_cols(re, im):
        r = re.shape[0]
        return jnp.stack([re.reshape(r, nt, lt), im.reshape(r, nt, lt)], axis=2).reshape(r, 2 * nt * lt)

    def tile_rows(re, im):
        c = re.shape[1]
        return jnp.stack([re.reshape(nt, lt, c), im.reshape(nt, lt, c)], axis=1).reshape(2 * nt * lt, c)

    bbd = jnp.stack([tile_cols(bd_in(bbar_re[i * gh:(i + 1) * gh]), bd_in(bbar_im[i * gh:(i + 1) * gh]))
                     for i in range(2)])
    cbd = jnp.stack([tile_rows(bd_out(c_re.astype(F32)[i * gh:(i + 1) * gh]),
                               -bd_out(c_im.astype(F32)[i * gh:(i + 1) * gh])) for i in range(2)])
    s = jnp.arange(t, dtype=F32)[:, None]
    ln_mag = (lr * dt).reshape(1, g * p)
    ang = (li * dt).reshape(1, g * p)
    pmag, nmag = jnp.exp(s * ln_mag), jnp.exp(-s * ln_mag)
    cs, sn = jnp.cos(s * ang), jnp.sin(s * ang)
    return dict(bbd=bbd.astype(BF16), cbd=cbd.astype(BF16),
                npre=nmag * cs, npim=-nmag * sn, ppre=pmag * cs, ppim=pmag * sn,
                are=a_re.reshape(1, g * p), aim=a_im.reshape(1, g * p))


def _even_layer(x, g_in, g_out, w_in, w_out, lam_re, lam_im, log_dt, b_re, b_im, c_re, c_im,
                d_skip, w_glu, sgu_g, sgu_w, sgu_b):
    bsz, seq, d = x.shape
    dh = d_skip.shape[0]
    t = min(MIX_CHUNK, seq)
    tb = _s5_tables(lam_re, lam_im, log_dt, b_re, b_im, c_re, c_im, t)
    ns = tb['npre'].shape[1]
    tri = jnp.tril(jnp.ones((t, t), F32)).astype(BF16)
    mask = jnp.tril(jnp.ones((SGU_CHUNK, SGU_CHUNK), dtype=bool))
    sguw = jnp.where(mask, sgu_w.astype(F32), 0.0).astype(BF16)
    hd = dh // SGU_HEADS
    sgub = jnp.repeat(sgu_b.astype(F32).T, hd, axis=1)
    xspec = pl.BlockSpec((1, t, d), lambda b, c: (b, c, 0))
    args = [x, g_in.reshape(1, d), g_out.reshape(1, d), w_in.astype(BF16), tb['bbd'], tb['cbd'],
            tb['npre'], tb['npim'], tb['ppre'], tb['ppim'], tb['are'], tb['aim'], tri,
            d_skip.reshape(1, dh).astype(F32), w_glu.astype(BF16), sgu_g.reshape(1, dh).astype(F32),
            sguw, sgub, w_out.astype(BF16)]
    return pl.pallas_call(
        _even_kernel,
        grid=(bsz, seq // t),
        in_specs=[xspec] + [_const_spec(a.shape) for a in args[1:]],
        out_specs=xspec,
        out_shape=jax.ShapeDtypeStruct(x.shape, F32),
        scratch_shapes=[pltpu.VMEM((1, ns), F32), pltpu.VMEM((1, ns), F32)],
        compiler_params=pltpu.CompilerParams(
            dimension_semantics=("arbitrary", "arbitrary"), vmem_limit_bytes=VMEM_LIMIT),
        name="even_mixer",
    )(*args)


def _odd_kernel(x_ref, gin_ref, gout_ref, win_ref, pm_ref, poolw_ref, pscale_ref, lb_ref,
                dall_ref, lvl_ref, onorm_ref, wout_ref, o_ref, halo_ref, st_ref):
    t = x_ref.shape[1]
    dh = lb_ref.shape[1]
    hd = HGRN_HEAD_DIM
    nheads = dh // hd
    blk = lvl_ref.shape[0]
    nb = t // blk
    nlev = int(math.log2(blk))

    @pl.when(pl.program_id(1) == 0)
    def _():
        halo_ref[...] = jnp.zeros_like(halo_ref)
        st_ref[...] = jnp.zeros_like(st_ref)

    x = x_ref[0]
    h = _rms(x, gin_ref[...]).astype(BF16)
    z = _dot(h, win_ref[...])

    zc = z[:, :dh].astype(BF16)
    zext = jnp.concatenate([halo_ref[...].astype(BF16), zc], axis=0)
    halo_ref[...] = zc[t - POOL_HALO:, :].astype(F32)
    gd = dh // len(POOL_WINDOWS)
    pooled = []
    later = jnp.minimum(pl.program_id(1), 1)
    for gi in range(len(POOL_WINDOWS)):
        m = _dot(pm_ref[later, gi], zext[:, gi * gd:(gi + 1) * gd])
        pooled.append(_dot(m.astype(BF16), poolw_ref[gi]))
    yc = jnp.concatenate(pooled, axis=1) * pscale_ref[...]

    lb = lb_ref[...]
    q = jax.nn.silu(z[:, dh:2 * dh])
    fg = lb + (1.0 - lb) * jax.nn.sigmoid(z[:, 2 * dh:3 * dh])
    logf = jnp.log(fg)
    k = 1.0 - fg
    v = z[:, 3 * dh:4 * dh].astype(BF16)
    hi = logf.astype(BF16)
    lo = (logf - hi.astype(F32)).astype(BF16)
    dall = dall_ref[...]
    gall = _dot(dall, hi) + _dot(dall, lo)
    b = gall[t:, :]
    lvl = lvl_ref[...]

    def level_decay(lev):
        if lev == 1:
            return jnp.exp(-jnp.abs(gall[:t, :]))
        n = 1 << lev
        parts = []
        for s0 in range(0, t, 2 * n):
            m = jnp.broadcast_to(b[s0 + n - 1:s0 + n, :], (2 * n, dh))
            parts.append(jnp.exp(-jnp.abs(b[s0:s0 + 2 * n, :] - m)))
        return jnp.concatenate(parts, axis=0) if len(parts) > 1 else parts[0]

    def block_scores(qa, ka):
        return [[_dot_nt(qa[j * blk:(j + 1) * blk, i * hd:(i + 1) * hd],
                         ka[j * blk:(j + 1) * blk, i * hd:(i + 1) * hd]) for j in range(nb)]
                for i in range(nheads)]

    kb = k.astype(BF16)
    scores = [[None] * nb for _ in range(nheads)]

    def lev_operands(i, _):
        if i == 0:
            return q.astype(BF16), kb
        if i == 1:
            return (q * fg).astype(BF16), kb
        e = level_decay(i - 1)
        return (q * e).astype(BF16), (k * e).astype(BF16)

    def lev_scores(i, ops):
        return block_scores(*ops)

    def lev_select(i, w):
        sel = lvl == (nlev if i == 0 else i - 1)
        for hh in range(nheads):
            for j in range(nb):
                scores[hh][j] = jnp.where(sel, w[hh][j], 0.0 if i == 0 else scores[hh][j])

    _skewed([lev_operands, lev_scores, lev_select], nlev + 1)

    cross = [None] * nb
    for j in range(1, nb):
        m = b[j * blk - 1:j * blk, :]
        qx = (q[j * blk:(j + 1) * blk, :] * jnp.exp(b[j * blk:(j + 1) * blk, :] - m)).astype(BF16)
        kx = (k[:j * blk, :] * jnp.exp(m - b[:j * blk, :])).astype(BF16)
        cross[j] = [_dot_nt(qx[:, i * hd:(i + 1) * hd], kx[:, i * hd:(i + 1) * hd]) for i in range(nheads)]

    blast = b[t - 1:t, :]
    qd = (q * jnp.exp(b)).astype(BF16)
    kd = (k * jnp.exp(blast - b)).astype(BF16)
    dec = jnp.exp(blast)
    outs = []
    for i in range(nheads):
        sl = slice(i * hd, (i + 1) * hd)
        st = st_ref[i]
        rows = []
        for j in range(nb):
            a = scores[i][j] if j == 0 else jnp.concatenate([cross[j][i], scores[i][j]], axis=1)
            rows.append(_dot(a.astype(BF16), v[:(j + 1) * blk, sl]))
        o = (jnp.concatenate(rows, axis=0) if nb > 1 else rows[0]) + _dot_nt(qd[:, sl], st.astype(BF16))
        st_ref[i] = st * dec[:, sl] + _dot_tn(v[:, sl], kd[:, sl])
        o = o * lax.rsqrt(jnp.mean(o * o, axis=-1, keepdims=True) + EPS) * onorm_ref[...]
        outs.append(o)
    yd = jnp.concatenate(outs, axis=1) * jax.nn.silu(z[:, 4 * dh:])

    ycat = jnp.concatenate([yc.astype(BF16), yd.astype(BF16)], axis=1)
    out = _dot(ycat, wout_ref[...])
    o_ref[0] = x + _rms(out, gout_ref[...])


def _pool_matrices(t):
    r = jnp.arange(t)[:, None]
    c = jnp.arange(t + POOL_HALO)[None, :] - POOL_HALO
    mats = []
    for first in (True, False):
        per = []
        for win in POOL_WINDOWS:
            band = (c <= r) & (c > r - win)
            if first:
                band = band & (c >= 0)
                cnt = jnp.minimum(r + 1, win).astype(F32)
            else:
                cnt = jnp.full((t, 1), float(win), F32)
            per.append(jnp.where(band, 1.0 / cnt, 0.0) - (c == r).astype(F32))
        mats.append(jnp.stack(per))
    return jnp.stack(mats).astype(BF16)


def _hgrn_level_matrices(t, blk):
    r = jnp.arange(t)[:, None]
    c = jnp.arange(t)[None, :]
    e = ((r >> 2) << 2) + 1
    upper = (c > e) & (c <= r)
    lower = (c > r) & (c <= e)
    mats = [upper.astype(F32) - lower.astype(F32), (c <= r).astype(F32)]
    nlev = int(math.log2(blk))
    r, c = r[:blk, :], c[:, :blk]
    x = r ^ c
    lvl = jnp.zeros((blk, blk), jnp.int32)
    for lev in range(1, nlev):
        lvl = lvl + (x >= (1 << lev)).astype(jnp.int32)
    lvl = jnp.where(r > c, lvl, jnp.where(r == c, nlev, -1))
    return jnp.concatenate(mats, axis=0).astype(BF16), lvl


def _odd_layer(x, g_in, g_out, w_in, w_out, pool_w, pool_scale, lb, onorm_g):
    bsz, seq, d = x.shape
    dh = lb.shape[0]
    t = min(MIX_CHUNK, seq)
    pm = _pool_matrices(t)
    dall, lvl = _hgrn_level_matrices(t, min(HGRN_BLOCK, t))
    nheads = dh // HGRN_HEAD_DIM
    xspec = pl.BlockSpec((1, t, d), lambda b, c: (b, c, 0))
    args = [x, g_in.reshape(1, d), g_out.reshape(1, d), w_in.astype(BF16), pm, pool_w.astype(BF16),
            pool_scale.reshape(1, dh).astype(F32), lb.reshape(1, dh).astype(F32), dall, lvl,
            onorm_g.reshape(1, HGRN_HEAD_DIM).astype(F32), w_out.astype(BF16)]
    return pl.pallas_call(
        _odd_kernel,
        grid=(bsz, seq // t),
        in_specs=[xspec] + [_const_spec(a.shape) for a in args[1:]],
        out_specs=xspec,
        out_shape=jax.ShapeDtypeStruct(x.shape, F32),
        scratch_shapes=[pltpu.VMEM((POOL_HALO, dh), F32),
                        pltpu.VMEM((nheads, HGRN_HEAD_DIM, HGRN_HEAD_DIM), F32)],
        compiler_params=pltpu.CompilerParams(
            dimension_semantics=("arbitrary", "arbitrary"), vmem_limit_bytes=VMEM_LIMIT),
        name="odd_mixer",
    )(*args)


def kernel(x, norm_g, ffn_wg, ffn_wu, ffn_wd, even_w_in, even_w_out, s5_lam_re, s5_lam_im, s5_log_dt, s5_b_re, s5_b_im, s5_c_re, s5_c_im, s5_d, s5_w_glu, sgu_norm_g, sgu_w, sgu_b, odd_w_in, odd_w_out, pool_w, pool_scale, hgrn_lb, hgrn_onorm_g):
    bsz, seq, d = x.shape
    depth = norm_g.shape[0]
    sm = jax.nn.softmax(hgrn_lb.astype(F32), axis=0)
    lb_all = jnp.cumsum(sm, axis=0) - sm[0:1]

    wg, wu, wd = ffn_wg.astype(BF16), ffn_wu.astype(BF16), ffn_wd.astype(BF16)

    def ffn(xx, li, half, g_in, g_out):
        y = _ffn(xx.reshape(bsz * seq, d), g_in, g_out, wg, wu, wd, li, half)
        return y.reshape(bsz, seq, d)

    for li in range(depth):
        g = norm_g[li].astype(F32)
        j = li // 2
        x = ffn(x, li, 0, g[0], g[1])
        if li % 2 == 0:
            x = _even_layer(x, g[2], g[3], even_w_in[j], even_w_out[j], s5_lam_re[j], s5_lam_im[j],
                            s5_log_dt[j], s5_b_re[j], s5_b_im[j], s5_c_re[j], s5_c_im[j], s5_d[j],
                            s5_w_glu[j], sgu_norm_g[j], sgu_w[j], sgu_b[j])
        else:
            x = _odd_layer(x, g[2], g[3], odd_w_in[j], odd_w_out[j], pool_w[j], pool_scale[j],
                           lb_all[li], hgrn_onorm_g[j])
        x = ffn(x, li, 1, g[4], g[5])
    return x
```

```python
import functools
import math

import jax
import jax.numpy as jnp
from jax import lax
from jax.experimental import pallas as pl
from jax.experimental.pallas import tpu as pltpu

F32 = jnp.float32
BF16 = jnp.bfloat16
EPS = 1e-6

S5_GROUP = 16
S5_STATE = 64
S5_LANE_TILE = 128
SGU_CHUNK = 128
SGU_HEADS = 4
POOL_WINDOWS = (2, 4, 8, 16)
POOL_HALO = 128
HGRN_HEAD_DIM = 128
HGRN_BLOCK = 128

FFN_TOKENS = 1024
FFN_SPLIT = 4
MIX_CHUNK = 256
VMEM_LIMIT = 56 * 1024 * 1024


def _rms(x, g):
    return x * lax.rsqrt(jnp.mean(x * x, axis=-1, keepdims=True) + EPS) * g


def _dot(a, b):
    return jnp.dot(a, b, preferred_element_type=F32)


def _dot_nt(a, b):
    return lax.dot_general(a, b, (((1,), (1,)), ((), ())), preferred_element_type=F32)


def _dot_tn(a, b):
    return lax.dot_general(a, b, (((0,), (0,)), ((), ())), preferred_element_type=F32)


def _skewed(stages, n, fillers=()):
    vals = [None] * n
    fillers = list(fillers)
    for step in range(n + len(stages) - 1):
        for k, stage in enumerate(stages):
            i = step - k
            if 0 <= i < n:
                vals[i] = stage(i, vals[i])
        if fillers:
            fillers.pop(0)()
    for f in fillers:
        f()
    return vals


def _const_spec(shape):
    nd = len(shape)
    return pl.BlockSpec(shape, lambda *_: (0,) * nd, pipeline_mode=pl.Buffered(1))


def _ffn_kernel(x_ref, gin_ref, gout_ref, wg_ref, wu_ref, wd_ref, *rest):
    o_ref = rest[len(rest) // 2]
    for src, dst in zip(rest[:len(rest) // 2], rest[len(rest) // 2 + 1:]):
        dst[...] = src[...].astype(BF16)
    hr = x_ref.shape[0] // FFN_SPLIT

    def rows(i):
        return slice(i * hr, (i + 1) * hr)

    def norm_in(i, _):
        return _rms(x_ref[rows(i), :], gin_ref[...]).astype(BF16)

    def up(i, h):
        return (jax.nn.silu(_dot(h, wg_ref[...])) * _dot(h, wu_ref[...])).astype(BF16)

    def down(i, p):
        return _dot(p, wd_ref[...])

    def norm_out(i, y):
        o_ref[rows(i), :] = x_ref[rows(i), :] + 0.5 * _rms(y, gout_ref[...])

    _skewed([norm_in, up, down, norm_out], FFN_SPLIT)


def _ffn(x2, g_in, g_out, w, stacks, nxt):
    n, d = x2.shape
    tm = min(FFN_TOKENS, n)
    steps = n // tm
    xspec = pl.BlockSpec((tm, d), lambda i: (i, 0))
    in_specs = [xspec, _const_spec((1, d)), _const_spec((1, d))] + [_const_spec(a.shape) for a in w]
    out_specs, out_shape, extra = [xspec], [jax.ShapeDtypeStruct((n, d), F32)], []
    if nxt is not None:
        for a in stacks:
            r, c = a.shape[2] // steps, a.shape[3]
            in_specs.append(pl.BlockSpec((None, None, r, c), lambda i: (nxt[0], nxt[1], i, 0)))
            out_specs.append(pl.BlockSpec((r, c), lambda i: (i, 0)))
            out_shape.append(jax.ShapeDtypeStruct(a.shape[2:], BF16))
            extra.append(a)
    res = pl.pallas_call(
        _ffn_kernel,
        grid=(steps,),
        in_specs=in_specs,
        out_specs=out_specs,
        out_shape=out_shape,
        compiler_params=pltpu.CompilerParams(
            dimension_semantics=("arbitrary",), vmem_limit_bytes=VMEM_LIMIT),
        name="ffn",
    )(x2, g_in.reshape(1, d), g_out.reshape(1, d), *w, *extra)
    return res[0], tuple(res[1:])


def _even_kernel(x_ref, gin_ref, gout_ref, win_ref, bbd_ref, cbd_ref, npre_ref, npim_ref,
                 ppre_ref, ppim_ref, are_ref, aim_ref, tri_ref, dskip_ref, wglu_ref,
                 sgug_ref, sguw_ref, sgub_ref, wout_ref, o_ref, cre_ref, cim_ref):
    t = x_ref.shape[1]
    dh = dskip_ref.shape[1]
    ns = npre_ref.shape[1]
    nh = ns // 2

    @pl.when(pl.program_id(1) == 0)
    def _():
        cre_ref[...] = jnp.zeros_like(cre_ref)
        cim_ref[...] = jnp.zeros_like(cim_ref)

    x = x_ref[0]
    h = _rms(x, gin_ref[...]).astype(BF16)
    z = _dot(h, win_ref[...])
    u = z[:, :dh]
    ub = u.astype(BF16)
    tri = tri_ref[...]

    lt = S5_LANE_TILE
    tph = nh // lt
    ys = [None, None]

    def tile_slices(i):
        hf, j = divmod(i, tph)
        return hf, slice(hf * nh + j * lt, hf * nh + (j + 1) * lt), slice(2 * j * lt, 2 * (j + 1) * lt)

    def s5_input(i, _):
        hf, _, ws = tile_slices(i)
        return _dot(ub[:, hf * (dh // 2):(hf + 1) * (dh // 2)], bbd_ref[hf, :, ws])

    def s5_prescale(i, bu):
        _, sl, _ = tile_slices(i)
        bre, bim = bu[:, :lt], bu[:, lt:]
        nr, ni = npre_ref[:, sl], npim_ref[:, sl]
        return jnp.concatenate([(nr * bre - ni * bim).astype(BF16),
                                (nr * bim + ni * bre).astype(BF16)], axis=1)

    def s5_prefix(i, zs):
        return _dot(tri, zs)

    def s5_state(i, acc):
        _, sl, _ = tile_slices(i)
        car, cai = cre_ref[:, sl], cim_ref[:, sl]
        ar, ai = are_ref[:, sl], aim_ref[:, sl]
        sre = acc[:, :lt] + (ar * car - ai * cai)
        sim = acc[:, lt:] + (ar * cai + ai * car)
        pr, pi = ppre_ref[:, sl], ppim_ref[:, sl]
        xre = pr * sre - pi * sim
        xim = pr * sim + pi * sre
        cre_ref[:, sl] = xre[t - 1:t, :]
        cim_ref[:, sl] = xim[t - 1:t, :]
        return jnp.concatenate([xre.astype(BF16), xim.astype(BF16)], axis=1)

    def s5_output(i, xs):
        hf, _, ws = tile_slices(i)
        part = _dot(xs, cbd_ref[hf, ws, :])
        ys[hf] = part if ys[hf] is None else ys[hf] + part

    _skewed([s5_input, s5_prescale, s5_prefix, s5_state, s5_output], 2 * tph)
    y = jnp.concatenate(ys, axis=1) + dskip_ref[...] * u
    y = jax.nn.gelu(y)
    ya = y * jax.nn.sigmoid(_dot(y.astype(BF16), wglu_ref[...]))

    gu = jax.nn.gelu(z[:, dh:2 * dh])
    gv = jax.nn.gelu(z[:, 2 * dh:])
    mu = jnp.mean(gv, axis=-1, keepdims=True)
    dv = gv - mu
    var = jnp.mean(dv * dv, axis=-1, keepdims=True)
    vn = (dv * lax.rsqrt(var + EPS) * sgug_ref[...]).astype(BF16)
    hd = dh // SGU_HEADS
    rows = []
    for c in range(t // SGU_CHUNK):
        cols = [_dot(sguw_ref[hh], vn[c * SGU_CHUNK:(c + 1) * SGU_CHUNK, hh * hd:(hh + 1) * hd])
                for hh in range(SGU_HEADS)]
        rows.append(jnp.concatenate(cols, axis=1) + sgub_ref[...])
    s = jnp.concatenate(rows, axis=0) if len(rows) > 1 else rows[0]
    yb = gu * s

    ycat = jnp.concatenate([ya.astype(BF16), yb.astype(BF16)], axis=1)
    out = _dot(ycat, wout_ref[...])
    o_ref[0] = x + _rms(out, gout_ref[...])


def _s5_tables(lam_re, lam_im, log_dt, b_re, b_im, c_re, c_im, t):
    g, p = lam_re.shape
    gh = g // 2
    lr = lam_re.astype(F32)
    li = lam_im.astype(F32)
    dt = jnp.exp(log_dt.astype(F32))[:, None]
    mag = jnp.exp(lr * dt)
    a_re = mag * jnp.cos(li * dt)
    a_im = mag * jnp.sin(li * dt)
    den = lr * lr + li * li
    coef_re = ((a_re - 1.0) * lr + a_im * li) / den
    coef_im = (a_im * lr - (a_re - 1.0) * li) / den
    br = b_re.astype(F32)
    bi = b_im.astype(F32)
    bbar_re = coef_re[..., None] * br - coef_im[..., None] * bi
    bbar_im = coef_re[..., None] * bi + coef_im[..., None] * br
    eye = jnp.eye(gh, dtype=F32)

    def bd_in(w):
        c = w.shape[2]
        return jnp.einsum('gpc,gh->gchp', w, eye).reshape(gh * c, gh * p)

    def bd_out(w):
        c = w.shape[1]
        return jnp.einsum('gcp,gh->gphc', w, eye).reshape(gh * p, gh * c)

    lt = S5_LANE_TILE
    nt = gh * p // lt

    def tile_cols(re, im):
        r = re.shape[0]
        return jnp.stack([re.reshape(r, nt, lt), im.reshape(r, nt, lt)], axis=2).reshape(r, 2 * nt * lt)

    def tile_rows(re, im):
        c = re.shape[1]
        return jnp.stack([re.reshape(nt, lt, c), im.reshape(nt, lt, c)], axis=1).reshape(2 * nt * lt, c)

    bbd = jnp.stack([tile_cols(bd_in(bbar_re[i * gh:(i + 1) * gh]), bd_in(bbar_im[i * gh:(i + 1) * gh]))
                     for i in range(2)])
    cbd = jnp.stack([tile_rows(bd_out(c_re.astype(F32)[i * gh:(i + 1) * gh]),
                               -bd_out(c_im.astype(F32)[i * gh:(i + 1) * gh])) for i in range(2)])
    s = jnp.arange(t, dtype=F32)[:, None]
    ln_mag = (lr * dt).reshape(1, g * p)
    ang = (li * dt).reshape(1, g * p)
    pmag, nmag = jnp.exp(s * ln_mag), jnp.exp(-s * ln_mag)
    cs, sn = jnp.cos(s * ang), jnp.sin(s * ang)
    return dict(bbd=bbd.astype(BF16), cbd=cbd.astype(BF16),
                npre=nmag * cs, npim=-nmag * sn, ppre=pmag * cs, ppim=pmag * sn,
                are=a_re.reshape(1, g * p), aim=a_im.reshape(1, g * p))


def _even_layer(x, g_in, g_out, w_in, w_out, lam_re, lam_im, log_dt, b_re, b_im, c_re, c_im,
                d_skip, w_glu, sgu_g, sgu_w, sgu_b):
    bsz, seq, d = x.shape
    dh = d_skip.shape[0]
    t = min(MIX_CHUNK, seq)
    tb = _s5_tables(lam_re, lam_im, log_dt, b_re, b_im, c_re, c_im, t)
    ns = tb['npre'].shape[1]
    tri = jnp.tril(jnp.ones((t, t), F32)).astype(BF16)
    mask = jnp.tril(jnp.ones((SGU_CHUNK, SGU_CHUNK), dtype=bool))
    sguw = jnp.where(mask, sgu_w.astype(F32), 0.0).astype(BF16)
    hd = dh // SGU_HEADS
    sgub = jnp.repeat(sgu_b.astype(F32).T, hd, axis=1)
    xspec = pl.BlockSpec((1, t, d), lambda b, c: (b, c, 0))
    args = [x, g_in.reshape(1, d), g_out.reshape(1, d), w_in.astype(BF16), tb['bbd'], tb['cbd'],
            tb['npre'], tb['npim'], tb['ppre'], tb['ppim'], tb['are'], tb['aim'], tri,
            d_skip.reshape(1, dh).astype(F32), w_glu.astype(BF16), sgu_g.reshape(1, dh).astype(F32),
            sguw, sgub, w_out.astype(BF16)]
    return pl.pallas_call(
        _even_kernel,
        grid=(bsz, seq // t),
        in_specs=[xspec] + [_const_spec(a.shape) for a in args[1:]],
        out_specs=xspec,
        out_shape=jax.ShapeDtypeStruct(x.shape, F32),
        scratch_shapes=[pltpu.VMEM((1, ns), F32), pltpu.VMEM((1, ns), F32)],
        compiler_params=pltpu.CompilerParams(
            dimension_semantics=("arbitrary", "arbitrary"), vmem_limit_bytes=VMEM_LIMIT),
        name="even_mixer",
    )(*args)


def _odd_kernel(x_ref, gin_ref, gout_ref, win_ref, pm_ref, poolw_ref, pscale_ref, lb_ref,
                dall_ref, lvl_ref, onorm_ref, wout_ref, o_ref, halo_ref, st_ref):
    t = x_ref.shape[1]
    dh = lb_ref.shape[1]
    hd = HGRN_HEAD_DIM
    nheads = dh // hd
    blk = lvl_ref.shape[0]
    nb = t // blk
    nlev = int(math.log2(blk))

    @pl.when(pl.program_id(1) == 0)
    def _():
        halo_ref[...] = jnp.zeros_like(halo_ref)
        st_ref[...] = jnp.zeros_like(st_ref)

    x = x_ref[0]
    h = _rms(x, gin_ref[...]).astype(BF16)
    z = _dot(h, win_ref[...])

    zc = z[:, :dh].astype(BF16)
    zext = jnp.concatenate([halo_ref[...].astype(BF16), zc], axis=0)
    halo_ref[...] = zc[t - POOL_HALO:, :].astype(F32)
    gd = dh // len(POOL_WINDOWS)
    pooled = []
    later = jnp.minimum(pl.program_id(1), 1)
    for gi in range(len(POOL_WINDOWS)):
        m = _dot(pm_ref[later, gi], zext[:, gi * gd:(gi + 1) * gd])
        pooled.append(_dot(m.astype(BF16), poolw_ref[gi]))
    yc = jnp.concatenate(pooled, axis=1) * pscale_ref[...]

    lb = lb_ref[...]
    q = jax.nn.silu(z[:, dh:2 * dh])
    fg = lb + (1.0 - lb) * jax.nn.sigmoid(z[:, 2 * dh:3 * dh])
    logf = jnp.log(fg)
    k = 1.0 - fg
    v = z[:, 3 * dh:4 * dh].astype(BF16)
    hi = logf.astype(BF16)
    lo = (logf - hi.astype(F32)).astype(BF16)
    dall = dall_ref[...]
    gall = _dot(dall, hi) + _dot(dall, lo)
    b = gall[t:, :]
    lvl = lvl_ref[...]

    def level_decay(lev):
        if lev == 1:
            return jnp.exp(-jnp.abs(gall[:t, :]))
        n = 1 << lev
        parts = []
        for s0 in range(0, t, 2 * n):
            m = jnp.broadcast_to(b[s0 + n - 1:s0 + n, :], (2 * n, dh))
            parts.append(jnp.exp(-jnp.abs(b[s0:s0 + 2 * n, :] - m)))
        return jnp.concatenate(parts, axis=0) if len(parts) > 1 else parts[0]

    def block_scores(qa, ka):
        return [[_dot_nt(qa[j * blk:(j + 1) * blk, i * hd:(i + 1) * hd],
                         ka[j * blk:(j + 1) * blk, i * hd:(i + 1) * hd]) for j in range(nb)]
                for i in range(nheads)]

    kb = k.astype(BF16)
    scores = [[None] * nb for _ in range(nheads)]

    def lev_operands(i, _):
        if i == 0:
            return q.astype(BF16), kb
        if i == 1:
            return (q * fg).astype(BF16), kb
        e = level_decay(i - 1)
        return (q * e).astype(BF16), (k * e).astype(BF16)

    def lev_scores(i, ops):
        return block_scores(*ops)

    def lev_select(i, w):
        sel = lvl == (nlev if i == 0 else i - 1)
        for hh in range(nheads):
            for j in range(nb):
                scores[hh][j] = jnp.where(sel, w[hh][j], 0.0 if i == 0 else scores[hh][j])

    _skewed([lev_operands, lev_scores, lev_select], nlev + 1)

    cross = [None] * nb
    for j in range(1, nb):
        m = b[j * blk - 1:j * blk, :]
        qx = (q[j * blk:(j + 1) * blk, :] * jnp.exp(b[j * blk:(j + 1) * blk, :] - m)).astype(BF16)
        kx = (k[:j * blk, :] * jnp.exp(m - b[:j * blk, :])).astype(BF16)
        cross[j] = [_dot_nt(qx[:, i * hd:(i + 1) * hd], kx[:, i * hd:(i + 1) * hd]) for i in range(nheads)]

    blast = b[t - 1:t, :]
    qd = (q * jnp.exp(b)).astype(BF16)
    kd = (k * jnp.exp(blast - b)).astype(BF16)
    dec = jnp.exp(blast)

    def head_out(i, _):
        sl = slice(i * hd, (i + 1) * hd)
        st = st_ref[i]
        rows = []
        for j in range(nb):
            a = scores[i][j] if j == 0 else jnp.concatenate([cross[j][i], scores[i][j]], axis=1)
            rows.append(_dot(a.astype(BF16), v[:(j + 1) * blk, sl]))
        o = (jnp.concatenate(rows, axis=0) if nb > 1 else rows[0]) + _dot_nt(qd[:, sl], st.astype(BF16))
        st_ref[i] = st * dec[:, sl] + _dot_tn(v[:, sl], kd[:, sl])
        return o

    def head_norm(i, o):
        return o * lax.rsqrt(jnp.mean(o * o, axis=-1, keepdims=True) + EPS) * onorm_ref[...]

    outs = _skewed([head_out, head_norm], nheads)
    yd = jnp.concatenate(outs, axis=1) * jax.nn.silu(z[:, 4 * dh:])

    ycat = jnp.concatenate([yc.astype(BF16), yd.astype(BF16)], axis=1)
    out = _dot(ycat, wout_ref[...])
    o_ref[0] = x + _rms(out, gout_ref[...])


def _pool_matrices(t):
    r = jnp.arange(t)[:, None]
    c = jnp.arange(t + POOL_HALO)[None, :] - POOL_HALO
    mats = []
    for first in (True, False):
        per = []
        for win in POOL_WINDOWS:
            band = (c <= r) & (c > r - win)
            if first:
                band = band & (c >= 0)
                cnt = jnp.minimum(r + 1, win).astype(F32)
            else:
                cnt = jnp.full((t, 1), float(win), F32)
            per.append(jnp.where(band, 1.0 / cnt, 0.0) - (c == r).astype(F32))
        mats.append(jnp.stack(per))
    return jnp.stack(mats).astype(BF16)


def _hgrn_level_matrices(t, blk):
    r = jnp.arange(t)[:, None]
    c = jnp.arange(t)[None, :]
    e = ((r >> 2) << 2) + 1
    upper = (c > e) & (c <= r)
    lower = (c > r) & (c <= e)
    mats = [upper.astype(F32) - lower.astype(F32), (c <= r).astype(F32)]
    nlev = int(math.log2(blk))
    r, c = r[:blk, :], c[:, :blk]
    x = r ^ c
    lvl = jnp.zeros((blk, blk), jnp.int32)
    for lev in range(1, nlev):
        lvl = lvl + (x >= (1 << lev)).astype(jnp.int32)
    lvl = jnp.where(r > c, lvl, jnp.where(r == c, nlev, -1))
    return jnp.concatenate(mats, axis=0).astype(BF16), lvl


def _odd_layer(x, g_in, g_out, w_in, w_out, pool_w, pool_scale, lb, onorm_g):
    bsz, seq, d = x.shape
    dh = lb.shape[0]
    t = min(MIX_CHUNK, seq)
    pm = _pool_matrices(t)
    dall, lvl = _hgrn_level_matrices(t, min(HGRN_BLOCK, t))
    nheads = dh // HGRN_HEAD_DIM
    xspec = pl.BlockSpec((1, t, d), lambda b, c: (b, c, 0))
    args = [x, g_in.reshape(1, d), g_out.reshape(1, d), w_in.astype(BF16), pm, pool_w.astype(BF16),
            pool_scale.reshape(1, dh).astype(F32), lb.reshape(1, dh).astype(F32), dall, lvl,
            onorm_g.reshape(1, HGRN_HEAD_DIM).astype(F32), w_out.astype(BF16)]
    return pl.pallas_call(
        _odd_kernel,
        grid=(bsz, seq // t),
        in_specs=[xspec] + [_const_spec(a.shape) for a in args[1:]],
        out_specs=xspec,
        out_shape=jax.ShapeDtypeStruct(x.shape, F32),
        scratch_shapes=[pltpu.VMEM((POOL_HALO, dh), F32),
                        pltpu.VMEM((nheads, HGRN_HEAD_DIM, HGRN_HEAD_DIM), F32)],
        compiler_params=pltpu.CompilerParams(
            dimension_semantics=("arbitrary", "arbitrary"), vmem_limit_bytes=VMEM_LIMIT),
        name="odd_mixer",
    )(*args)


def kernel(x, norm_g, ffn_wg, ffn_wu, ffn_wd, even_w_in, even_w_out, s5_lam_re, s5_lam_im, s5_log_dt, s5_b_re, s5_b_im, s5_c_re, s5_c_im, s5_d, s5_w_glu, sgu_norm_g, sgu_w, sgu_b, odd_w_in, odd_w_out, pool_w, pool_scale, hgrn_lb, hgrn_onorm_g):
    bsz, seq, d = x.shape
    depth = norm_g.shape[0]
    sm = jax.nn.softmax(hgrn_lb.astype(F32), axis=0)
    lb_all = jnp.cumsum(sm, axis=0) - sm[0:1]

    stacks = (ffn_wg, ffn_wu, ffn_wd)
    order = [(li, half) for li in range(depth) for half in range(2)]
    state = {'w': tuple(a[0, 0].astype(BF16) for a in stacks), 'i': 0}

    def ffn(xx, g_in, g_out):
        i = state['i']
        nxt = order[i + 1] if i + 1 < len(order) else None
        y, state['w'] = _ffn(xx.reshape(bsz * seq, d), g_in, g_out, state['w'], stacks, nxt)
        state['i'] = i + 1
        return y.reshape(bsz, seq, d)

    for li in range(depth):
        g = norm_g[li].astype(F32)
        j = li // 2
        x = ffn(x, g[0], g[1])
        if li % 2 == 0:
            x = _even_layer(x, g[2], g[3], even_w_in[j], even_w_out[j], s5_lam_re[j], s5_lam_im[j],
                            s5_log_dt[j], s5_b_re[j], s5_b_im[j], s5_c_re[j], s5_c_im[j], s5_d[j],
                            s5_w_glu[j], sgu_norm_g[j], sgu_w[j], sgu_b[j])
        else:
            x = _odd_layer(x, g[2], g[3], odd_w_in[j], odd_w_out[j], pool_w[j], pool_scale[j],
                           lb_all[li], hgrn_onorm_g[j])
        x = ffn(x, g[4], g[5])
    return x
```

```python
import functools
import math

import jax
import jax.numpy as jnp
from jax import lax
from jax.experimental import pallas as pl
from jax.experimental.pallas import tpu as pltpu

F32 = jnp.float32
BF16 = jnp.bfloat16
EPS = 1e-6

S5_GROUP = 16
S5_STATE = 64
S5_LANE_TILE = 128
SGU_CHUNK = 128
SGU_HEADS = 4
POOL_WINDOWS = (2, 4, 8, 16)
POOL_HALO = 128
HGRN_HEAD_DIM = 128
HGRN_BLOCK = 128

FFN_TOKENS = 1024
FFN_SPLIT = 4
MIX_CHUNK = 256
MIX_BATCH = 2
VMEM_LIMIT = 56 * 1024 * 1024


def _rms(x, g):
    return x * lax.rsqrt(jnp.mean(x * x, axis=-1, keepdims=True) + EPS) * g


def _dot(a, b):
    return jnp.dot(a, b, preferred_element_type=F32)


def _dot_nt(a, b):
    return lax.dot_general(a, b, (((1,), (1,)), ((), ())), preferred_element_type=F32)


def _dot_tn(a, b):
    return lax.dot_general(a, b, (((0,), (0,)), ((), ())), preferred_element_type=F32)


def _skewed(stages, n, fillers=()):
    vals = [None] * n
    fillers = list(fillers)
    for step in range(n + len(stages) - 1):
        for k, stage in enumerate(stages):
            i = step - k
            if 0 <= i < n:
                vals[i] = stage(i, vals[i])
        if fillers:
            fillers.pop(0)()
    for f in fillers:
        f()
    return vals


def _skewed_phases(make, n):
    live = []
    for step in range(n):
        live.insert(0, make(step))
        live = [g for g in live if next(g, True) is None]
    while live:
        live = [g for g in live if next(g, True) is None]


def _const_spec(shape):
    nd = len(shape)
    return pl.BlockSpec(shape, lambda *_: (0,) * nd, pipeline_mode=pl.Buffered(1))


def _ffn_kernel(x_ref, gin_ref, gout_ref, wg_ref, wu_ref, wd_ref, *rest):
    o_ref = rest[len(rest) // 2]
    for src, dst in zip(rest[:len(rest) // 2], rest[len(rest) // 2 + 1:]):
        dst[...] = src[...].astype(BF16)
    hr = x_ref.shape[0] // FFN_SPLIT

    def rows(i):
        return slice(i * hr, (i + 1) * hr)

    def norm_in(i, _):
        return _rms(x_ref[rows(i), :], gin_ref[...]).astype(BF16)

    def up(i, h):
        return (jax.nn.silu(_dot(h, wg_ref[...])) * _dot(h, wu_ref[...])).astype(BF16)

    def down(i, p):
        return _dot(p, wd_ref[...])

    def norm_out(i, y):
        o_ref[rows(i), :] = x_ref[rows(i), :] + 0.5 * _rms(y, gout_ref[...])

    _skewed([norm_in, up, down, norm_out], FFN_SPLIT)


def _ffn(x2, g_in, g_out, w, stacks, nxt):
    n, d = x2.shape
    tm = min(FFN_TOKENS, n)
    steps = n // tm
    xspec = pl.BlockSpec((tm, d), lambda i: (i, 0))
    in_specs = [xspec, _const_spec((1, d)), _const_spec((1, d))] + [_const_spec(a.shape) for a in w]
    out_specs, out_shape, extra = [xspec], [jax.ShapeDtypeStruct((n, d), F32)], []
    if nxt is not None:
        for a in stacks:
            r, c = a.shape[2] // steps, a.shape[3]
            in_specs.append(pl.BlockSpec((None, None, r, c), lambda i: (nxt[0], nxt[1], i, 0)))
            out_specs.append(pl.BlockSpec((r, c), lambda i: (i, 0)))
            out_shape.append(jax.ShapeDtypeStruct(a.shape[2:], BF16))
            extra.append(a)
    res = pl.pallas_call(
        _ffn_kernel,
        grid=(steps,),
        in_specs=in_specs,
        out_specs=out_specs,
        out_shape=out_shape,
        compiler_params=pltpu.CompilerParams(
            dimension_semantics=("arbitrary",), vmem_limit_bytes=VMEM_LIMIT),
        name="ffn",
    )(x2, g_in.reshape(1, d), g_out.reshape(1, d), *w, *extra)
    return res[0], tuple(res[1:])


def _even_kernel(x_ref, gin_ref, gout_ref, win_ref, bbd_ref, cbd_ref, npre_ref, npim_ref,
                 ppre_ref, ppim_ref, are_ref, aim_ref, tri_ref, dskip_ref, wglu_ref,
                 sgug_ref, sguw_ref, sgub_ref, wout_ref, o_ref, cre_ref, cim_ref):
    t = x_ref.shape[1]
    dh = dskip_ref.shape[1]
    ns = npre_ref.shape[1]
    nh = ns // 2
    lt = S5_LANE_TILE
    tph = nh // lt

    @pl.when(pl.program_id(1) == 0)
    def _():
        cre_ref[...] = jnp.zeros_like(cre_ref)
        cim_ref[...] = jnp.zeros_like(cim_ref)

    def project(bi, _):
        x = x_ref[bi]
        h = _rms(x, gin_ref[...]).astype(BF16)
        return x, _dot(h, win_ref[...])

    def s5(bi, xz):
        x, z = xz
        u = z[:, :dh]
        ub = u.astype(BF16)
        tri = tri_ref[...]
        ys = [None, None]

        def tile_slices(i):
            hf, j = divmod(i, tph)
            return hf, slice(hf * nh + j * lt, hf * nh + (j + 1) * lt), slice(2 * j * lt, 2 * (j + 1) * lt)

        def s5_input(i, _):
            hf, _, ws = tile_slices(i)
            return _dot(ub[:, hf * (dh // 2):(hf + 1) * (dh // 2)], bbd_ref[hf, :, ws])

        def s5_prescale(i, bu):
            _, sl, _ = tile_slices(i)
            bre, bim = bu[:, :lt], bu[:, lt:]
            nr, ni = npre_ref[:, sl], npim_ref[:, sl]
            return jnp.concatenate([(nr * bre - ni * bim).astype(BF16),
                                    (nr * bim + ni * bre).astype(BF16)], axis=1)

        def s5_prefix(i, zs):
            return _dot(tri, zs)

        def s5_state(i, acc):
            _, sl, _ = tile_slices(i)
            car, cai = cre_ref[bi, :, sl], cim_ref[bi, :, sl]
            ar, ai = are_ref[:, sl], aim_ref[:, sl]
            sre = acc[:, :lt] + (ar * car - ai * cai)
            sim = acc[:, lt:] + (ar * cai + ai * car)
            pr, pi = ppre_ref[:, sl], ppim_ref[:, sl]
            xre = pr * sre - pi * sim
            xim = pr * sim + pi * sre
            cre_ref[bi, :, sl] = xre[t - 1:t, :]
            cim_ref[bi, :, sl] = xim[t - 1:t, :]
            return jnp.concatenate([xre.astype(BF16), xim.astype(BF16)], axis=1)

        def s5_output(i, xs):
            hf, _, ws = tile_slices(i)
            part = _dot(xs, cbd_ref[hf, ws, :])
            ys[hf] = part if ys[hf] is None else ys[hf] + part

        _skewed([s5_input, s5_prescale, s5_prefix, s5_state, s5_output], 2 * tph)
        return x, z, jnp.concatenate(ys, axis=1) + dskip_ref[...] * u

    def finish(bi, xzy):
        x, z, y = xzy
        y = jax.nn.gelu(y)
        ya = y * jax.nn.sigmoid(_dot(y.astype(BF16), wglu_ref[...]))

        gu = jax.nn.gelu(z[:, dh:2 * dh])
        gv = jax.nn.gelu(z[:, 2 * dh:])
        mu = jnp.mean(gv, axis=-1, keepdims=True)
        dv = gv - mu
        var = jnp.mean(dv * dv, axis=-1, keepdims=True)
        vn = (dv * lax.rsqrt(var + EPS) * sgug_ref[...]).astype(BF16)
        hd = dh // SGU_HEADS
        rows = []
        for c in range(t // SGU_CHUNK):
            cols = [_dot(sguw_ref[hh], vn[c * SGU_CHUNK:(c + 1) * SGU_CHUNK, hh * hd:(hh + 1) * hd])
                    for hh in range(SGU_HEADS)]
            rows.append(jnp.concatenate(cols, axis=1) + sgub_ref[...])
        s = jnp.concatenate(rows, axis=0) if len(rows) > 1 else rows[0]
        yb = gu * s

        ycat = jnp.concatenate([ya.astype(BF16), yb.astype(BF16)], axis=1)
        out = _dot(ycat, wout_ref[...])
        o_ref[bi] = x + _rms(out, gout_ref[...])

    _skewed([project, s5, finish], x_ref.shape[0])


def _s5_tables(lam_re, lam_im, log_dt, b_re, b_im, c_re, c_im, t):
    g, p = lam_re.shape
    gh = g // 2
    lr = lam_re.astype(F32)
    li = lam_im.astype(F32)
    dt = jnp.exp(log_dt.astype(F32))[:, None]
    mag = jnp.exp(lr * dt)
    a_re = mag * jnp.cos(li * dt)
    a_im = mag * jnp.sin(li * dt)
    den = lr * lr + li * li
    coef_re = ((a_re - 1.0) * lr + a_im * li) / den
    coef_im = (a_im * lr - (a_re - 1.0) * li) / den
    br = b_re.astype(F32)
    bi = b_im.astype(F32)
    bbar_re = coef_re[..., None] * br - coef_im[..., None] * bi
    bbar_im = coef_re[..., None] * bi + coef_im[..., None] * br
    eye = jnp.eye(gh, dtype=F32)

    def bd_in(w):
        c = w.shape[2]
        return jnp.einsum('gpc,gh->gchp', w, eye).reshape(gh * c, gh * p)

    def bd_out(w):
        c = w.shape[1]
        return jnp.einsum('gcp,gh->gphc', w, eye).reshape(gh * p, gh * c)

    lt = S5_LANE_TILE
    nt = gh * p // lt

    def tile_cols(re, im):
        r = re.shape[0]
        return jnp.stack([re.reshape(r, nt, lt), im.reshape(r, nt, lt)], axis=2).reshape(r, 2 * nt * lt)

    def tile_rows(re, im):
        c = re.shape[1]
        return jnp.stack([re.reshape(nt, lt, c), im.reshape(nt, lt, c)], axis=1).reshape(2 * nt * lt, c)

    bbd = jnp.stack([tile_cols(bd_in(bbar_re[i * gh:(i + 1) * gh]), bd_in(bbar_im[i * gh:(i + 1) * gh]))
                     for i in range(2)])
    cbd = jnp.stack([tile_rows(bd_out(c_re.astype(F32)[i * gh:(i + 1) * gh]),
                               -bd_out(c_im.astype(F32)[i * gh:(i + 1) * gh])) for i in range(2)])
    s = jnp.arange(t, dtype=F32)[:, None]
    ln_mag = (lr * dt).reshape(1, g * p)
    ang = (li * dt).reshape(1, g * p)
    pmag, nmag = jnp.exp(s * ln_mag), jnp.exp(-s * ln_mag)
    cs, sn = jnp.cos(s * ang), jnp.sin(s * ang)
    return dict(bbd=bbd.astype(BF16), cbd=cbd.astype(BF16),
                npre=nmag * cs, npim=-nmag * sn, ppre=pmag * cs, ppim=pmag * sn,
                are=a_re.reshape(1, g * p), aim=a_im.reshape(1, g * p))


def _even_layer(x, g_in, g_out, w_in, w_out, lam_re, lam_im, log_dt, b_re, b_im, c_re, c_im,
                d_skip, w_glu, sgu_g, sgu_w, sgu_b):
    bsz, seq, d = x.shape
    dh = d_skip.shape[0]
    t = min(MIX_CHUNK, seq)
    tb = _s5_tables(lam_re, lam_im, log_dt, b_re, b_im, c_re, c_im, t)
    ns = tb['npre'].shape[1]
    tri = jnp.tril(jnp.ones((t, t), F32)).astype(BF16)
    mask = jnp.tril(jnp.ones((SGU_CHUNK, SGU_CHUNK), dtype=bool))
    sguw = jnp.where(mask, sgu_w.astype(F32), 0.0).astype(BF16)
    hd = dh // SGU_HEADS
    sgub = jnp.repeat(sgu_b.astype(F32).T, hd, axis=1)
    nbat = min(MIX_BATCH, bsz)
    xspec = pl.BlockSpec((nbat, t, d), lambda b, c: (b, c, 0))
    args = [x, g_in.reshape(1, d), g_out.reshape(1, d), w_in.astype(BF16), tb['bbd'], tb['cbd'],
            tb['npre'], tb['npim'], tb['ppre'], tb['ppim'], tb['are'], tb['aim'], tri,
            d_skip.reshape(1, dh).astype(F32), w_glu.astype(BF16), sgu_g.reshape(1, dh).astype(F32),
            sguw, sgub, w_out.astype(BF16)]
    return pl.pallas_call(
        _even_kernel,
        grid=(bsz // nbat, seq // t),
        in_specs=[xspec] + [_const_spec(a.shape) for a in args[1:]],
        out_specs=xspec,
        out_shape=jax.ShapeDtypeStruct(x.shape, F32),
        scratch_shapes=[pltpu.VMEM((nbat, 1, ns), F32), pltpu.VMEM((nbat, 1, ns), F32)],
        compiler_params=pltpu.CompilerParams(
            dimension_semantics=("arbitrary", "arbitrary"), vmem_limit_bytes=VMEM_LIMIT),
        name="even_mixer",
    )(*args)


def _odd_kernel(x_ref, *refs):
    halo_ref, st_ref = refs[-2:]

    @pl.when(pl.program_id(1) == 0)
    def _():
        halo_ref[...] = jnp.zeros_like(halo_ref)
        st_ref[...] = jnp.zeros_like(st_ref)

    _skewed_phases(functools.partial(_odd_sequence, x_ref, *refs), x_ref.shape[0])


def _odd_sequence(x_ref, gin_ref, gout_ref, win_ref, pm_ref, poolw_ref, pscale_ref, lb_ref,
                  dall_ref, lvl_ref, onorm_ref, wout_ref, o_ref, halo_ref, st_ref, bi):
    t = x_ref.shape[1]
    dh = lb_ref.shape[1]
    hd = HGRN_HEAD_DIM
    nheads = dh // hd
    blk = lvl_ref.shape[0]
    nb = t // blk
    nlev = int(math.log2(blk))

    x = x_ref[bi]
    h = _rms(x, gin_ref[...]).astype(BF16)
    z = _dot(h, win_ref[...])
    yield

    zc = z[:, :dh].astype(BF16)
    zext = jnp.concatenate([halo_ref[bi].astype(BF16), zc], axis=0)
    halo_ref[bi] = zc[t - POOL_HALO:, :].astype(F32)
    gd = dh // len(POOL_WINDOWS)
    pooled = []
    later = jnp.minimum(pl.program_id(1), 1)
    for gi in range(len(POOL_WINDOWS)):
        m = _dot(pm_ref[later, gi], zext[:, gi * gd:(gi + 1) * gd])
        pooled.append(_dot(m.astype(BF16), poolw_ref[gi]))
    yc = jnp.concatenate(pooled, axis=1) * pscale_ref[...]

    lb = lb_ref[...]
    q = jax.nn.silu(z[:, dh:2 * dh])
    fg = lb + (1.0 - lb) * jax.nn.sigmoid(z[:, 2 * dh:3 * dh])
    logf = jnp.log(fg)
    k = 1.0 - fg
    v = z[:, 3 * dh:4 * dh].astype(BF16)
    hi = logf.astype(BF16)
    lo = (logf - hi.astype(F32)).astype(BF16)
    dall = dall_ref[...]
    gall = _dot(dall, hi) + _dot(dall, lo)
    b = gall[t:, :]
    lvl = lvl_ref[...]

    def level_decay(lev):
        if lev == 1:
            return jnp.exp(-jnp.abs(gall[:t, :]))
        n = 1 << lev
        parts = []
        for s0 in range(0, t, 2 * n):
            m = jnp.broadcast_to(b[s0 + n - 1:s0 + n, :], (2 * n, dh))
            parts.append(jnp.exp(-jnp.abs(b[s0:s0 + 2 * n, :] - m)))
        return jnp.concatenate(parts, axis=0) if len(parts) > 1 else parts[0]

    def block_scores(qa, ka):
        return [[_dot_nt(qa[j * blk:(j + 1) * blk, i * hd:(i + 1) * hd],
                         ka[j * blk:(j + 1) * blk, i * hd:(i + 1) * hd]) for j in range(nb)]
                for i in range(nheads)]

    kb = k.astype(BF16)
    scores = [[None] * nb for _ in range(nheads)]

    def lev_operands(i, _):
        if i == 0:
            return q.astype(BF16), kb
        if i == 1:
            return (q * fg).astype(BF16), kb
        e = level_decay(i - 1)
        return (q * e).astype(BF16), (k * e).astype(BF16)

    def lev_scores(i, ops):
        return block_scores(*ops)

    def lev_select(i, w):
        sel = lvl == (nlev if i == 0 else i - 1)
        for hh in range(nheads):
            for j in range(nb):
                scores[hh][j] = jnp.where(sel, w[hh][j], 0.0 if i == 0 else scores[hh][j])

    _skewed([lev_operands, lev_scores, lev_select], nlev + 1)

    cross = [None] * nb
    for j in range(1, nb):
        m = b[j * blk - 1:j * blk, :]
        qx = (q[j * blk:(j + 1) * blk, :] * jnp.exp(b[j * blk:(j + 1) * blk, :] - m)).astype(BF16)
        kx = (k[:j * blk, :] * jnp.exp(m - b[:j * blk, :])).astype(BF16)
        cross[j] = [_dot_nt(qx[:, i * hd:(i + 1) * hd], kx[:, i * hd:(i + 1) * hd]) for i in range(nheads)]

    blast = b[t - 1:t, :]
    qd = (q * jnp.exp(b)).astype(BF16)
    kd = (k * jnp.exp(blast - b)).astype(BF16)
    dec = jnp.exp(blast)
    yield

    def head_out(i, _):
        sl = slice(i * hd, (i + 1) * hd)
        st = st_ref[bi, i]
        rows = []
        for j in range(nb):
            a = scores[i][j] if j == 0 else jnp.concatenate([cross[j][i], scores[i][j]], axis=1)
            rows.append(_dot(a.astype(BF16), v[:(j + 1) * blk, sl]))
        o = (jnp.concatenate(rows, axis=0) if nb > 1 else rows[0]) + _dot_nt(qd[:, sl], st.astype(BF16))
        st_ref[bi, i] = st * dec[:, sl] + _dot_tn(v[:, sl], kd[:, sl])
        return o

    def head_norm(i, o):
        return o * lax.rsqrt(jnp.mean(o * o, axis=-1, keepdims=True) + EPS) * onorm_ref[...]

    outs = _skewed([head_out, head_norm], nheads)
    yd = jnp.concatenate(outs, axis=1) * jax.nn.silu(z[:, 4 * dh:])

    ycat = jnp.concatenate([yc.astype(BF16), yd.astype(BF16)], axis=1)
    out = _dot(ycat, wout_ref[...])
    o_ref[bi] = x + _rms(out, gout_ref[...])


def _pool_matrices(t):
    r = jnp.arange(t)[:, None]
    c = jnp.arange(t + POOL_HALO)[None, :] - POOL_HALO
    mats = []
    for first in (True, False):
        per = []
        for win in POOL_WINDOWS:
            band = (c <= r) & (c > r - win)
            if first:
                band = band & (c >= 0)
                cnt = jnp.minimum(r + 1, win).astype(F32)
            else:
                cnt = jnp.full((t, 1), float(win), F32)
            per.append(jnp.where(band, 1.0 / cnt, 0.0) - (c == r).astype(F32))
        mats.append(jnp.stack(per))
    return jnp.stack(mats).astype(BF16)


def _hgrn_level_matrices(t, blk):
    r = jnp.arange(t)[:, None]
    c = jnp.arange(t)[None, :]
    e = ((r >> 2) << 2) + 1
    upper = (c > e) & (c <= r)
    lower = (c > r) & (c <= e)
    mats = [upper.astype(F32) - lower.astype(F32), (c <= r).astype(F32)]
    nlev = int(math.log2(blk))
    r, c = r[:blk, :], c[:, :blk]
    x = r ^ c
    lvl = jnp.zeros((blk, blk), jnp.int32)
    for lev in range(1, nlev):
        lvl = lvl + (x >= (1 << lev)).astype(jnp.int32)
    lvl = jnp.where(r > c, lvl, jnp.where(r == c, nlev, -1))
    return jnp.concatenate(mats, axis=0).astype(BF16), lvl


def _odd_layer(x, g_in, g_out, w_in, w_out, pool_w, pool_scale, lb, onorm_g):
    bsz, seq, d = x.shape
    dh = lb.shape[0]
    t = min(MIX_CHUNK, seq)
    pm = _pool_matrices(t)
    dall, lvl = _hgrn_level_matrices(t, min(HGRN_BLOCK, t))
    nheads = dh // HGRN_HEAD_DIM
    nbat = min(MIX_BATCH, bsz)
    xspec = pl.BlockSpec((nbat, t, d), lambda b, c: (b, c, 0))
    args = [x, g_in.reshape(1, d), g_out.reshape(1, d), w_in.astype(BF16), pm, pool_w.astype(BF16),
            pool_scale.reshape(1, dh).astype(F32), lb.reshape(1, dh).astype(F32), dall, lvl,
            onorm_g.reshape(1, HGRN_HEAD_DIM).astype(F32), w_out.astype(BF16)]
    return pl.pallas_call(
        _odd_kernel,
        grid=(bsz // nbat, seq // t),
        in_specs=[xspec] + [_const_spec(a.shape) for a in args[1:]],
        out_specs=xspec,
        out_shape=jax.ShapeDtypeStruct(x.shape, F32),
        scratch_shapes=[pltpu.VMEM((nbat, POOL_HALO, dh), F32),
                        pltpu.VMEM((nbat, nheads, HGRN_HEAD_DIM, HGRN_HEAD_DIM), F32)],
        compiler_params=pltpu.CompilerParams(
            dimension_semantics=("arbitrary", "arbitrary"), vmem_limit_bytes=VMEM_LIMIT),
        name="odd_mixer",
    )(*args)


def kernel(x, norm_g, ffn_wg, ffn_wu, ffn_wd, even_w_in, even_w_out, s5_lam_re, s5_lam_im, s5_log_dt, s5_b_re, s5_b_im, s5_c_re, s5_c_im, s5_d, s5_w_glu, sgu_norm_g, sgu_w, sgu_b, odd_w_in, odd_w_out, pool_w, pool_scale, hgrn_lb, hgrn_onorm_g):
    bsz, seq, d = x.shape
    depth = norm_g.shape[0]
    sm = jax.nn.softmax(hgrn_lb.astype(F32), axis=0)
    lb_all = jnp.cumsum(sm, axis=0) - sm[0:1]

    stacks = (ffn_wg, ffn_wu, ffn_wd)
    order = [(li, half) for li in range(depth) for half in range(2)]
    state = {'w': tuple(a[0, 0].astype(BF16) for a in stacks), 'i': 0}

    def ffn(xx, g_in, g_out):
        i = state['i']
        nxt = order[i + 1] if i + 1 < len(order) else None
        y, state['w'] = _ffn(xx.reshape(bsz * seq, d), g_in, g_out, state['w'], stacks, nxt)
        state['i'] = i + 1
        return y.reshape(bsz, seq, d)

    for li in range(depth):
        g = norm_g[li].astype(F32)
        j = li // 2
        x = ffn(x, g[0], g[1])
        if li % 2 == 0:
            x = _even_layer(x, g[2], g[3], even_w_in[j], even_w_out[j], s5_lam_re[j], s5_lam_im[j],
                            s5_log_dt[j], s5_b_re[j], s5_b_im[j], s5_c_re[j], s5_c_im[j], s5_d[j],
                            s5_w_glu[j], sgu_norm_g[j], sgu_w[j], sgu_b[j])
        else:
            x = _odd_layer(x, g[2], g[3], odd_w_in[j], odd_w_out[j], pool_w[j], pool_scale[j],
                           lb_all[li], hgrn_onorm_g[j])
        x = ffn(x, g[4], g[5])
    return x
```

```python
import functools
import math

import jax
import jax.numpy as jnp
import numpy as np
from jax import lax
from jax.experimental import pallas as pl
from jax.experimental.pallas import tpu as pltpu

F32 = jnp.float32
BF16 = jnp.bfloat16
EPS = 1e-6
LOG2_E = 1.4426950408889634

S5_LANE_TILE = 128
SGU_CHUNK = 128
SGU_HEADS = 4
POOL_WINDOWS = (2, 4, 8, 16)
POOL_HALO = 128
HGRN_HEAD_DIM = 128
HGRN_BLOCK = 128

FFN_TOKENS = 1024
FFN_SPLIT = 4
MIX_CHUNK = 256
MIX_BATCH = 4
VMEM_LIMIT = 56 * 1024 * 1024


def _rms(x, g):
    return x * lax.rsqrt(jnp.mean(x * x, axis=-1, keepdims=True) + EPS) * g


def _dot(a, b):
    return jnp.dot(a, b, preferred_element_type=F32)


def _dot_nt(a, b):
    return lax.dot_general(a, b, (((1,), (1,)), ((), ())), preferred_element_type=F32)


def _dot_tn(a, b):
    return lax.dot_general(a, b, (((0,), (0,)), ((), ())), preferred_element_type=F32)


def _skewed(stages, n):
    vals = [None] * n
    for step in range(n + len(stages) - 1):
        for k, stage in enumerate(stages):
            i = step - k
            if 0 <= i < n:
                vals[i] = stage(i, vals[i])
    return vals


def _skewed_phases(make, n):
    live = []
    for step in range(n):
        live.insert(0, make(step))
        live = [g for g in live if next(g, True) is None]
    while live:
        live = [g for g in live if next(g, True) is None]


def _const_spec(shape):
    nd = len(shape)
    return pl.BlockSpec(shape, lambda *_: (0,) * nd, pipeline_mode=pl.Buffered(1))


def _ffn_kernel(x_ref, gin_ref, gout_ref, wg_ref, wu_ref, wd_ref, *rest):
    o_ref = rest[len(rest) // 2]
    for src, dst in zip(rest[:len(rest) // 2], rest[len(rest) // 2 + 1:]):
        dst[...] = src[...].astype(BF16)
    hr = x_ref.shape[0] // FFN_SPLIT

    def rows(i):
        return slice(i * hr, (i + 1) * hr)

    def norm_in(i, _):
        return _rms(x_ref[rows(i), :], gin_ref[...]).astype(BF16)

    def up(i, h):
        return (jax.nn.silu(_dot(h, wg_ref[...])) * _dot(h, wu_ref[...])).astype(BF16)

    def down(i, p):
        return _dot(p, wd_ref[...])

    def norm_out(i, y):
        o_ref[rows(i), :] = x_ref[rows(i), :] + 0.5 * _rms(y, gout_ref[...])

    _skewed([norm_in, up, down, norm_out], FFN_SPLIT)


def _ffn(x2, g_in, g_out, w, side):
    n, d = x2.shape
    tm = min(FFN_TOKENS, n)
    steps = n // tm
    xspec = pl.BlockSpec((tm, d), lambda i: (i, 0))
    in_specs = [xspec, _const_spec((1, d)), _const_spec((1, d))] + [_const_spec(a.shape) for a in w]
    out_specs, out_shape = [xspec], [jax.ShapeDtypeStruct((n, d), F32)]
    for a, lead in side:
        r, c = a.shape[-2] // steps, a.shape[-1]
        in_specs.append(pl.BlockSpec((None,) * len(lead) + (r, c), lambda i, lead=lead: lead + (i, 0)))
        out_specs.append(pl.BlockSpec((r, c), lambda i: (i, 0)))
        out_shape.append(jax.ShapeDtypeStruct(a.shape[-2:], BF16))
    res = pl.pallas_call(
        _ffn_kernel,
        grid=(steps,),
        in_specs=in_specs,
        out_specs=out_specs,
        out_shape=out_shape,
        compiler_params=pltpu.CompilerParams(
            dimension_semantics=("arbitrary",), vmem_limit_bytes=VMEM_LIMIT),
        name="ffn",
    )(x2, g_in.reshape(1, d), g_out.reshape(1, d), *w, *[a for a, _ in side])
    return res[0], list(res[1:])


def _even_kernel(x_ref, gin_ref, gout_ref, win_ref, bbd_ref, cbd_ref, npre_ref, npim_ref,
                 ppre_ref, ppim_ref, are_ref, aim_ref, tri_ref, dskip_ref, wglu_ref,
                 sgug_ref, sguw_ref, sgub_ref, wout_ref, o_ref, cre_ref, cim_ref):
    t = x_ref.shape[1]
    dh = dskip_ref.shape[1]
    ns = npre_ref.shape[1]
    nh = ns // 2
    lt = S5_LANE_TILE
    tph = nh // lt

    @pl.when(pl.program_id(1) == 0)
    def _():
        cre_ref[...] = jnp.zeros_like(cre_ref)
        cim_ref[...] = jnp.zeros_like(cim_ref)

    def project(bi, _):
        x = x_ref[bi]
        h = _rms(x, gin_ref[...]).astype(BF16)
        return x, _dot(h, win_ref[...])

    def s5(bi, xz):
        x, z = xz
        u = z[:, :dh]
        ub = u.astype(BF16)
        tri = tri_ref[...]
        ys = [None, None]

        def tile_slices(i):
            hf, j = divmod(i, tph)
            return hf, slice(hf * nh + j * lt, hf * nh + (j + 1) * lt), slice(2 * j * lt, 2 * (j + 1) * lt)

        def s5_input(i, _):
            hf, _, ws = tile_slices(i)
            return _dot(ub[:, hf * (dh // 2):(hf + 1) * (dh // 2)], bbd_ref[hf, :, ws])

        def s5_prescale(i, bu):
            _, sl, _ = tile_slices(i)
            bre, bim = bu[:, :lt], bu[:, lt:]
            nr, ni = npre_ref[:, sl], npim_ref[:, sl]
            return jnp.concatenate([(nr * bre - ni * bim).astype(BF16),
                                    (nr * bim + ni * bre).astype(BF16)], axis=1)

        def s5_prefix(i, zs):
            return _dot(tri, zs)

        def s5_state(i, acc):
            _, sl, _ = tile_slices(i)
            car, cai = cre_ref[bi, :, sl], cim_ref[bi, :, sl]
            ar, ai = are_ref[:, sl], aim_ref[:, sl]
            sre = acc[:, :lt] + (ar * car - ai * cai)
            sim = acc[:, lt:] + (ar * cai + ai * car)
            pr, pi = ppre_ref[:, sl], ppim_ref[:, sl]
            xre = pr * sre - pi * sim
            xim = pr * sim + pi * sre
            cre_ref[bi, :, sl] = xre[t - 1:t, :]
            cim_ref[bi, :, sl] = xim[t - 1:t, :]
            return jnp.concatenate([xre.astype(BF16), xim.astype(BF16)], axis=1)

        def s5_output(i, xs):
            hf, _, ws = tile_slices(i)
            part = _dot(xs, cbd_ref[hf, ws, :])
            ys[hf] = part if ys[hf] is None else ys[hf] + part

        _skewed([s5_input, s5_prescale, s5_prefix, s5_state, s5_output], 2 * tph)
        return x, z, jnp.concatenate(ys, axis=1) + dskip_ref[...] * u

    def finish(bi, xzy):
        x, z, y = xzy
        y = jax.nn.gelu(y)
        ya = y * jax.nn.sigmoid(_dot(y.astype(BF16), wglu_ref[...]))

        gu = jax.nn.gelu(z[:, dh:2 * dh])
        gv = jax.nn.gelu(z[:, 2 * dh:])
        mu = jnp.mean(gv, axis=-1, keepdims=True)
        dv = gv - mu
        var = jnp.mean(dv * dv, axis=-1, keepdims=True)
        vn = (dv * lax.rsqrt(var + EPS) * sgug_ref[...]).astype(BF16)
        hd = dh // SGU_HEADS
        rows = []
        for c in range(t // SGU_CHUNK):
            cols = [_dot(sguw_ref[hh], vn[c * SGU_CHUNK:(c + 1) * SGU_CHUNK, hh * hd:(hh + 1) * hd])
                    for hh in range(SGU_HEADS)]
            rows.append(jnp.concatenate(cols, axis=1) + sgub_ref[...])
        s = jnp.concatenate(rows, axis=0) if len(rows) > 1 else rows[0]
        yb = gu * s

        ycat = jnp.concatenate([ya.astype(BF16), yb.astype(BF16)], axis=1)
        out = _dot(ycat, wout_ref[...])
        o_ref[bi] = x + _rms(out, gout_ref[...])

    _skewed([project, s5, finish], x_ref.shape[0])


def _s5_tables(lam_re, lam_im, log_dt, b_re, b_im, c_re, c_im, t):
    g, p = lam_re.shape
    gh = g // 2
    lr = lam_re.astype(F32)
    li = lam_im.astype(F32)
    dt = jnp.exp(log_dt.astype(F32))[:, None]
    mag = jnp.exp(lr * dt)
    a_re = mag * jnp.cos(li * dt)
    a_im = mag * jnp.sin(li * dt)
    den = lr * lr + li * li
    coef_re = ((a_re - 1.0) * lr + a_im * li) / den
    coef_im = (a_im * lr - (a_re - 1.0) * li) / den
    br = b_re.astype(F32)
    bi = b_im.astype(F32)
    bbar_re = coef_re[..., None] * br - coef_im[..., None] * bi
    bbar_im = coef_re[..., None] * bi + coef_im[..., None] * br
    eye = jnp.eye(gh, dtype=F32)

    def bd_in(w):
        c = w.shape[2]
        return jnp.einsum('gpc,gh->gchp', w, eye).reshape(gh * c, gh * p)

    def bd_out(w):
        c = w.shape[1]
        return jnp.einsum('gcp,gh->gphc', w, eye).reshape(gh * p, gh * c)

    lt = S5_LANE_TILE
    nt = gh * p // lt

    def tile_cols(re, im):
        r = re.shape[0]
        return jnp.stack([re.reshape(r, nt, lt), im.reshape(r, nt, lt)], axis=2).reshape(r, 2 * nt * lt)

    def tile_rows(re, im):
        c = re.shape[1]
        return jnp.stack([re.reshape(nt, lt, c), im.reshape(nt, lt, c)], axis=1).reshape(2 * nt * lt, c)

    bbd = jnp.stack([tile_cols(bd_in(bbar_re[i * gh:(i + 1) * gh]), bd_in(bbar_im[i * gh:(i + 1) * gh]))
                     for i in range(2)])
    cbd = jnp.stack([tile_rows(bd_out(c_re.astype(F32)[i * gh:(i + 1) * gh]),
                               -bd_out(c_im.astype(F32)[i * gh:(i + 1) * gh])) for i in range(2)])
    r = 1 << (int(math.log2(t)) // 2)
    c = t // 2
    ln_mag = (lr * dt).reshape(1, g * p)
    ang = (li * dt).reshape(1, g * p)
    hi = jnp.arange(t // r, dtype=F32)[:, None] * float(r) - float(c)
    lo = jnp.arange(r, dtype=F32)[:, None]
    pmag = (jnp.exp(hi * ln_mag)[:, None] * jnp.exp(lo * ln_mag)[None]).reshape(t, g * p)
    nmag = (jnp.exp(-hi * ln_mag)[:, None] * jnp.exp(-lo * ln_mag)[None]).reshape(t, g * p)
    chi, shi = jnp.cos(hi * ang)[:, None], jnp.sin(hi * ang)[:, None]
    clo, slo = jnp.cos(lo * ang)[None], jnp.sin(lo * ang)[None]
    cs = (chi * clo - shi * slo).reshape(t, g * p)
    sn = (shi * clo + chi * slo).reshape(t, g * p)
    cmag = jnp.exp(float(c + 1) * ln_mag)
    return dict(bbd=bbd.astype(BF16), cbd=cbd.astype(BF16),
                npre=nmag * cs, npim=-nmag * sn, ppre=pmag * cs, ppim=pmag * sn,
                are=cmag * jnp.cos(float(c + 1) * ang), aim=cmag * jnp.sin(float(c + 1) * ang))


def _even_layer(x, g_in, g_out, w_in, w_out, lam_re, lam_im, log_dt, b_re, b_im, c_re, c_im,
                d_skip, w_glu, sgu_g, sgu_w, sgu_b):
    bsz, seq, d = x.shape
    dh = d_skip.shape[0]
    t = min(MIX_CHUNK, seq)
    tb = _s5_tables(lam_re, lam_im, log_dt, b_re, b_im, c_re, c_im, t)
    ns = tb['npre'].shape[1]
    tri = jnp.asarray(np.tril(np.ones((t, t), np.float32)), BF16)
    mask = np.tril(np.ones((SGU_CHUNK, SGU_CHUNK), dtype=bool))
    sguw = jnp.where(mask, sgu_w.astype(F32), 0.0).astype(BF16)
    hd = dh // SGU_HEADS
    sgub = jnp.repeat(sgu_b.astype(F32).T, hd, axis=1)
    nbat = min(MIX_BATCH, bsz)
    xspec = pl.BlockSpec((nbat, t, d), lambda b, c: (b, c, 0))
    args = [x, g_in.reshape(1, d), g_out.reshape(1, d), w_in, tb['bbd'], tb['cbd'],
            tb['npre'], tb['npim'], tb['ppre'], tb['ppim'], tb['are'], tb['aim'], tri,
            d_skip.reshape(1, dh).astype(F32), w_glu, sgu_g.reshape(1, dh).astype(F32),
            sguw, sgub, w_out]
    return pl.pallas_call(
        _even_kernel,
        grid=(bsz // nbat, seq // t),
        in_specs=[xspec] + [_const_spec(a.shape) for a in args[1:]],
        out_specs=xspec,
        out_shape=jax.ShapeDtypeStruct(x.shape, F32),
        scratch_shapes=[pltpu.VMEM((nbat, 1, ns), F32), pltpu.VMEM((nbat, 1, ns), F32)],
        compiler_params=pltpu.CompilerParams(
            dimension_semantics=("arbitrary", "arbitrary"), vmem_limit_bytes=VMEM_LIMIT),
        name="even_mixer",
    )(*args)


def _odd_kernel(x_ref, *refs):
    halo_ref, st_ref = refs[-2:]

    @pl.when(pl.program_id(1) == 0)
    def _():
        halo_ref[...] = jnp.zeros_like(halo_ref)
        st_ref[...] = jnp.zeros_like(st_ref)

    _skewed_phases(functools.partial(_odd_sequence, x_ref, *refs), x_ref.shape[0])


def _odd_sequence(x_ref, gin_ref, gout_ref, win_ref, pm_ref, poolw_ref, pscale_ref, lb_ref,
                  dall_ref, lvl_ref, onorm_ref, wout_ref, o_ref, halo_ref, st_ref, bi):
    t = x_ref.shape[1]
    dh = lb_ref.shape[1]
    hd = HGRN_HEAD_DIM
    nheads = dh // hd
    blk = lvl_ref.shape[0]
    nb = t // blk
    nlev = int(math.log2(blk))

    x = x_ref[bi]
    h = _rms(x, gin_ref[...]).astype(BF16)
    z = _dot(h, win_ref[...])
    yield

    zc = z[:, :dh].astype(BF16)
    zext = jnp.concatenate([halo_ref[bi].astype(BF16), zc], axis=0)
    halo_ref[bi] = zc[t - POOL_HALO:, :].astype(F32)
    gd = dh // len(POOL_WINDOWS)
    pooled = []
    later = jnp.minimum(pl.program_id(1), 1)
    for gi in range(len(POOL_WINDOWS)):
        m = _dot(pm_ref[later, gi], zext[:, gi * gd:(gi + 1) * gd])
        pooled.append(_dot(m.astype(BF16), poolw_ref[gi]))
    yc = jnp.concatenate(pooled, axis=1) * pscale_ref[...]

    lb = lb_ref[...]
    q = jax.nn.silu(z[:, dh:2 * dh])
    fg = lb + (1.0 - lb) * jax.nn.sigmoid(z[:, 2 * dh:3 * dh])
    logf = jnp.log(fg) * LOG2_E
    k = 1.0 - fg
    v = z[:, 3 * dh:4 * dh].astype(BF16)
    hi = logf.astype(BF16)
    lo = (logf - hi.astype(F32)).astype(BF16)
    dall = dall_ref[...]
    gall = _dot(dall, hi) + _dot(dall, lo)
    b = gall[t:, :]
    lvl = lvl_ref[...]

    def level_decay(lev):
        if lev == 1:
            return jnp.exp2(-jnp.abs(gall[:t, :]))
        n = 1 << lev
        parts = []
        for s0 in range(0, t, 2 * n):
            row = b[s0 + n - 1:s0 + n, :]
            if n % 8 == 0:
                m = jnp.broadcast_to(row, (n, dh))
                parts += [jnp.exp2(m - b[s0:s0 + n, :]), jnp.exp2(b[s0 + n:s0 + 2 * n, :] - m)]
            else:
                parts.append(jnp.exp2(-jnp.abs(b[s0:s0 + 2 * n, :] - jnp.broadcast_to(row, (2 * n, dh)))))
        return jnp.concatenate(parts, axis=0)

    def block_scores(qa, ka):
        return [[_dot_nt(qa[j * blk:(j + 1) * blk, i * hd:(i + 1) * hd],
                         ka[j * blk:(j + 1) * blk, i * hd:(i + 1) * hd]) for j in range(nb)]
                for i in range(nheads)]

    qb, kb = q.astype(BF16), k.astype(BF16)
    scores = [[None] * nb for _ in range(nheads)]

    def lev_decay(i, _):
        if i == 0:
            return None
        if i == 1:
            return fg.astype(BF16)
        return level_decay(i - 1).astype(BF16)

    def lev_operands(i, e):
        if i == 0:
            return qb, kb
        return qb * e, (kb if i == 1 else kb * e)

    def lev_scores(i, ops):
        return block_scores(*ops)

    def lev_select(i, w):
        sel = lvl == (nlev if i == 0 else i - 1)
        for hh in range(nheads):
            for j in range(nb):
                scores[hh][j] = jnp.where(sel, w[hh][j], 0.0 if i == 0 else scores[hh][j])

    _skewed([lev_decay, lev_operands, lev_scores, lev_select], nlev + 1)

    cross = [None] * nb
    for j in range(1, nb):
        m = b[j * blk - 1:j * blk, :]
        qx = (q[j * blk:(j + 1) * blk, :] * jnp.exp2(b[j * blk:(j + 1) * blk, :] - m)).astype(BF16)
        kx = (k[:j * blk, :] * jnp.exp2(m - b[:j * blk, :])).astype(BF16)
        cross[j] = [_dot_nt(qx[:, i * hd:(i + 1) * hd], kx[:, i * hd:(i + 1) * hd]) for i in range(nheads)]

    blast = b[t - 1:t, :]
    qd = (q * jnp.exp2(b)).astype(BF16)
    kd = (k * jnp.exp2(blast - b)).astype(BF16)
    dec = jnp.exp2(blast)
    yield

    def head_out(i, _):
        sl = slice(i * hd, (i + 1) * hd)
        st = st_ref[bi, i]
        rows = []
        for j in range(nb):
            a = scores[i][j] if j == 0 else jnp.concatenate([cross[j][i], scores[i][j]], axis=1)
            rows.append(_dot(a.astype(BF16), v[:(j + 1) * blk, sl]))
        o = (jnp.concatenate(rows, axis=0) if nb > 1 else rows[0]) + _dot_nt(qd[:, sl], st.astype(BF16))
        st_ref[bi, i] = st * dec[:, sl] + _dot_tn(v[:, sl], kd[:, sl])
        return o

    def head_norm(i, o):
        return o * lax.rsqrt(jnp.mean(o * o, axis=-1, keepdims=True) + EPS) * onorm_ref[...]

    outs = _skewed([head_out, head_norm], nheads)
    yd = jnp.concatenate(outs, axis=1) * jax.nn.silu(z[:, 4 * dh:])

    ycat = jnp.concatenate([yc.astype(BF16), yd.astype(BF16)], axis=1)
    out = _dot(ycat, wout_ref[...])
    o_ref[bi] = x + _rms(out, gout_ref[...])


def _pool_matrices(t):
    r = jnp.arange(t)[:, None]
    c = jnp.arange(t + POOL_HALO)[None, :] - POOL_HALO
    mats = []
    for first in (True, False):
        per = []
        for win in POOL_WINDOWS:
            band = (c <= r) & (c > r - win)
            if first:
                band = band & (c >= 0)
                cnt = jnp.minimum(r + 1, win).astype(F32)
            else:
                cnt = jnp.full((t, 1), float(win), F32)
            per.append(jnp.where(band, 1.0 / cnt, 0.0) - (c == r).astype(F32))
        mats.append(jnp.stack(per))
    return jnp.stack(mats).astype(BF16)


def _hgrn_level_matrices(t, blk):
    r = np.arange(t)[:, None]
    c = np.arange(t)[None, :]
    e = ((r >> 2) << 2) + 1
    upper = (c > e) & (c <= r)
    lower = (c > r) & (c <= e)
    mats = [upper.astype(np.float32) - lower.astype(np.float32), (c <= r).astype(np.float32)]
    nlev = int(math.log2(blk))
    r, c = r[:blk, :], c[:, :blk]
    x = r ^ c
    lvl = np.zeros((blk, blk), np.int32)
    for lev in range(1, nlev):
        lvl = lvl + (x >= (1 << lev)).astype(np.int32)
    lvl = np.where(r > c, lvl, np.where(r == c, nlev, -1)).astype(np.int32)
    return jnp.asarray(np.concatenate(mats, axis=0), BF16), jnp.asarray(lvl)


def _odd_layer(x, g_in, g_out, w_in, w_out, pool_w, pool_scale, lb, onorm_g):
    bsz, seq, d = x.shape
    dh = lb.shape[0]
    t = min(MIX_CHUNK, seq)
    pm = _pool_matrices(t)
    dall, lvl = _hgrn_level_matrices(t, min(HGRN_BLOCK, t))
    nheads = dh // HGRN_HEAD_DIM
    nbat = min(MIX_BATCH, bsz)
    xspec = pl.BlockSpec((nbat, t, d), lambda b, c: (b, c, 0))
    args = [x, g_in.reshape(1, d), g_out.reshape(1, d), w_in, pm, pool_w,
            pool_scale.reshape(1, dh).astype(F32), lb.reshape(1, dh).astype(F32), dall, lvl,
            onorm_g.reshape(1, HGRN_HEAD_DIM).astype(F32), w_out]
    return pl.pallas_call(
        _odd_kernel,
        grid=(bsz // nbat, seq // t),
        in_specs=[xspec] + [_const_spec(a.shape) for a in args[1:]],
        out_specs=xspec,
        out_shape=jax.ShapeDtypeStruct(x.shape, F32),
        scratch_shapes=[pltpu.VMEM((nbat, POOL_HALO, dh), F32),
                        pltpu.VMEM((nbat, nheads, HGRN_HEAD_DIM, HGRN_HEAD_DIM), F32)],
        compiler_params=pltpu.CompilerParams(
            dimension_semantics=("arbitrary", "arbitrary"), vmem_limit_bytes=VMEM_LIMIT),
        name="odd_mixer",
    )(*args)


def kernel(x, norm_g, ffn_wg, ffn_wu, ffn_wd, even_w_in, even_w_out, s5_lam_re, s5_lam_im, s5_log_dt, s5_b_re, s5_b_im, s5_c_re, s5_c_im, s5_d, s5_w_glu, sgu_norm_g, sgu_w, sgu_b, odd_w_in, odd_w_out, pool_w, pool_scale, hgrn_lb, hgrn_onorm_g):
    bsz, seq, d = x.shape
    depth = norm_g.shape[0]
    sm = jax.nn.softmax(hgrn_lb.astype(F32), axis=0)
    lb_all = jnp.cumsum(sm, axis=0) - sm[0:1]

    stacks = (ffn_wg, ffn_wu, ffn_wd)
    pool_w2 = pool_w.reshape(pool_w.shape[0], -1, pool_w.shape[-1])
    order = [(li, half) for li in range(depth) for half in range(2)]
    state = {'w': [a[0, 0].astype(BF16) for a in stacks], 'i': 0}

    def ffn(xx, g_in, g_out, mixer_weights=()):
        i = state['i']
        side = [(a, order[i + 1]) for a in stacks] if i + 1 < len(order) else []
        y, cast = _ffn(xx.reshape(bsz * seq, d), g_in, g_out, state['w'], side + list(mixer_weights))
        state['w'], state['i'] = cast[:len(side)], i + 1
        return y.reshape(bsz, seq, d), cast[len(side):]

    for li in range(depth):
        g = norm_g[li].astype(F32)
        j = li // 2
        if li % 2 == 0:
            x, (w_in, w_out, w_glu) = ffn(x, g[0], g[1], [(even_w_in, (j,)), (even_w_out, (j,)), (s5_w_glu, (j,))])
            x = _even_layer(x, g[2], g[3], w_in, w_out, s5_lam_re[j], s5_lam_im[j],
                            s5_log_dt[j], s5_b_re[j], s5_b_im[j], s5_c_re[j], s5_c_im[j], s5_d[j],
                            w_glu, sgu_norm_g[j], sgu_w[j], sgu_b[j])
        else:
            x, (w_in, w_out, w_pool) = ffn(x, g[0], g[1], [(odd_w_in, (j,)), (odd_w_out, (j,)), (pool_w2, (j,))])
            x = _odd_layer(x, g[2], g[3], w_in, w_out, w_pool.reshape(pool_w.shape[1:]), pool_scale[j],
                           lb_all[li], hgrn_onorm_g[j])
        x, _ = ffn(x, g[4], g[5])
    return x
```

```python
import functools
import math

import jax
import jax.numpy as jnp
import numpy as np
from jax import lax
from jax.experimental import pallas as pl
from jax.experimental.pallas import tpu as pltpu

F32 = jnp.float32
BF16 = jnp.bfloat16
EPS = 1e-6
LOG2_E = 1.4426950408889634

S5_LANE_TILE = 128
SGU_CHUNK = 128
SGU_HEADS = 4
POOL_WINDOWS = (2, 4, 8, 16)
POOL_HALO = 128
HGRN_HEAD_DIM = 128
HGRN_BLOCK = 128

FFN_TOKENS = 1024
FFN_SPLIT = 4
MIX_CHUNK = 256
MIX_BATCH = 4
VMEM_LIMIT = 56 * 1024 * 1024
FFN_FUSED_TOKENS = 512
FFN_FUSED_SPLIT = 2
VMEM_LIMIT_FUSED = 60 * 1024 * 1024


def _rms(x, g):
    return x * lax.rsqrt(jnp.mean(x * x, axis=-1, keepdims=True) + EPS) * g


def _dot(a, b):
    return jnp.dot(a, b, preferred_element_type=F32)


def _dot_nt(a, b):
    return lax.dot_general(a, b, (((1,), (1,)), ((), ())), preferred_element_type=F32)


def _dot_tn(a, b):
    return lax.dot_general(a, b, (((0,), (0,)), ((), ())), preferred_element_type=F32)


def _skewed(stages, n):
    vals = [None] * n
    for step in range(n + len(stages) - 1):
        for k, stage in enumerate(stages):
            i = step - k
            if 0 <= i < n:
                vals[i] = stage(i, vals[i])
    return vals


def _skewed_phases(make, n):
    live = []
    for step in range(n):
        live.insert(0, make(step))
        live = [g for g in live if next(g, True) is None]
    while live:
        live = [g for g in live if next(g, True) is None]


def _const_spec(shape):
    nd = len(shape)
    return pl.BlockSpec(shape, lambda *_: (0,) * nd, pipeline_mode=pl.Buffered(1))


def _ffn_kernel(nffn, split, x_ref, *refs):
    rest = refs[5 * nffn:]
    o_ref = rest[len(rest) // 2]
    for src, dst in zip(rest[:len(rest) // 2], rest[len(rest) // 2 + 1:]):
        dst[...] = src[...].astype(BF16)
    hr = x_ref.shape[0] // split

    def rows(i):
        return slice(i * hr, (i + 1) * hr)

    stages = []
    for f in range(nffn):
        gin_ref, gout_ref, wg_ref, wu_ref, wd_ref = refs[5 * f:5 * f + 5]

        def norm_in(i, x, f=f, gin_ref=gin_ref):
            x = x_ref[rows(i), :] if f == 0 else x
            return x, _rms(x, gin_ref[...]).astype(BF16)

        def up(i, xh, wg_ref=wg_ref, wu_ref=wu_ref):
            x, h = xh
            return x, (jax.nn.silu(_dot(h, wg_ref[...])) * _dot(h, wu_ref[...])).astype(BF16)

        def down(i, xp, wd_ref=wd_ref):
            return xp[0], _dot(xp[1], wd_ref[...])

        def norm_out(i, xy, f=f, gout_ref=gout_ref):
            x = xy[0] + 0.5 * _rms(xy[1], gout_ref[...])
            if f == nffn - 1:
                o_ref[rows(i), :] = x
            return x

        stages += [norm_in, up, down, norm_out]

    _skewed(stages, split)


def _ffn(x2, ffns, side):
    n, d = x2.shape
    fused = len(ffns) > 1
    tm = min(FFN_FUSED_TOKENS if fused else FFN_TOKENS, n)
    split = FFN_FUSED_SPLIT if fused else FFN_SPLIT
    steps = n // tm
    xspec = pl.BlockSpec((tm, d), lambda i: (i, 0))
    in_specs, args = [xspec], [x2]
    for g_in, g_out, w in ffns:
        in_specs += [_const_spec((1, d)), _const_spec((1, d))] + [_const_spec(a.shape) for a in w]
        args += [g_in.reshape(1, d), g_out.reshape(1, d), *w]
    out_specs, out_shape = [xspec], [jax.ShapeDtypeStruct((n, d), F32)]
    for a, lead in side:
        r, c = a.shape[-2] // steps, a.shape[-1]
        in_specs.append(pl.BlockSpec((None,) * len(lead) + (r, c), lambda i, lead=lead: lead + (i, 0)))
        out_specs.append(pl.BlockSpec((r, c), lambda i: (i, 0)))
        out_shape.append(jax.ShapeDtypeStruct(a.shape[-2:], BF16))
    res = pl.pallas_call(
        functools.partial(_ffn_kernel, len(ffns), split),
        grid=(steps,),
        in_specs=in_specs,
        out_specs=out_specs,
        out_shape=out_shape,
        compiler_params=pltpu.CompilerParams(
            dimension_semantics=("arbitrary",), vmem_limit_bytes=VMEM_LIMIT_FUSED),
        name="ffn",
    )(*args, *[a for a, _ in side])
    return res[0], list(res[1:])


def _even_kernel(x_ref, gin_ref, gout_ref, win_ref, bbd_ref, cbd_ref, npre_ref, npim_ref,
                 ppre_ref, ppim_ref, are_ref, aim_ref, tri_ref, dskip_ref, wglu_ref,
                 sgug_ref, sguw_ref, sgub_ref, wout_ref, o_ref, cre_ref, cim_ref):
    t = x_ref.shape[1]
    dh = dskip_ref.shape[1]
    ns = npre_ref.shape[1]
    nh = ns // 2
    lt = S5_LANE_TILE
    tph = nh // lt

    @pl.when(pl.program_id(1) == 0)
    def _():
        cre_ref[...] = jnp.zeros_like(cre_ref)
        cim_ref[...] = jnp.zeros_like(cim_ref)

    def project(bi, _):
        x = x_ref[bi]
        h = _rms(x, gin_ref[...]).astype(BF16)
        return x, _dot(h, win_ref[...])

    def s5(bi, xz):
        x, z = xz
        u = z[:, :dh]
        ub = u.astype(BF16)
        tri = tri_ref[...]
        ys = [None, None]

        def tile_slices(i):
            hf, j = divmod(i, tph)
            return hf, slice(hf * nh + j * lt, hf * nh + (j + 1) * lt), slice(2 * j * lt, 2 * (j + 1) * lt)

        def s5_input(i, _):
            hf, _, ws = tile_slices(i)
            return _dot(ub[:, hf * (dh // 2):(hf + 1) * (dh // 2)], bbd_ref[hf, :, ws])

        def s5_prescale(i, bu):
            _, sl, _ = tile_slices(i)
            bre, bim = bu[:, :lt], bu[:, lt:]
            nr, ni = npre_ref[:, sl], npim_ref[:, sl]
            return jnp.concatenate([(nr * bre - ni * bim).astype(BF16),
                                    (nr * bim + ni * bre).astype(BF16)], axis=1)

        def s5_prefix(i, zs):
            return _dot(tri, zs)

        def s5_state(i, acc):
            _, sl, _ = tile_slices(i)
            car, cai = cre_ref[bi, :, sl], cim_ref[bi, :, sl]
            ar, ai = are_ref[:, sl], aim_ref[:, sl]
            sre = acc[:, :lt] + (ar * car - ai * cai)
            sim = acc[:, lt:] + (ar * cai + ai * car)
            pr, pi = ppre_ref[:, sl], ppim_ref[:, sl]
            xre = pr * sre - pi * sim
            xim = pr * sim + pi * sre
            cre_ref[bi, :, sl] = xre[t - 1:t, :]
            cim_ref[bi, :, sl] = xim[t - 1:t, :]
            return jnp.concatenate([xre.astype(BF16), xim.astype(BF16)], axis=1)

        def s5_output(i, xs):
            hf, _, ws = tile_slices(i)
            part = _dot(xs, cbd_ref[hf, ws, :])
            ys[hf] = part if ys[hf] is None else ys[hf] + part

        _skewed([s5_input, s5_prescale, s5_prefix, s5_state, s5_output], 2 * tph)
        return x, z, jnp.concatenate(ys, axis=1) + dskip_ref[...] * u

    def finish(bi, xzy):
        x, z, y = xzy
        y = jax.nn.gelu(y)
        ya = y * jax.nn.sigmoid(_dot(y.astype(BF16), wglu_ref[...]))

        gu = jax.nn.gelu(z[:, dh:2 * dh])
        gv = jax.nn.gelu(z[:, 2 * dh:])
        mu = jnp.mean(gv, axis=-1, keepdims=True)
        dv = gv - mu
        var = jnp.mean(dv * dv, axis=-1, keepdims=True)
        vn = (dv * lax.rsqrt(var + EPS) * sgug_ref[...]).astype(BF16)
        hd = dh // SGU_HEADS
        rows = []
        for c in range(t // SGU_CHUNK):
            cols = [_dot(sguw_ref[hh], vn[c * SGU_CHUNK:(c + 1) * SGU_CHUNK, hh * hd:(hh + 1) * hd])
                    for hh in range(SGU_HEADS)]
            rows.append(jnp.concatenate(cols, axis=1) + sgub_ref[...])
        s = jnp.concatenate(rows, axis=0) if len(rows) > 1 else rows[0]
        yb = gu * s

        ycat = jnp.concatenate([ya.astype(BF16), yb.astype(BF16)], axis=1)
        out = _dot(ycat, wout_ref[...])
        o_ref[bi] = x + _rms(out, gout_ref[...])

    _skewed([project, s5, finish], x_ref.shape[0])


def _s5_tables(lam_re, lam_im, log_dt, b_re, b_im, c_re, c_im, t):
    g, p = lam_re.shape
    gh = g // 2
    lr = lam_re.astype(F32)
    li = lam_im.astype(F32)
    dt = jnp.exp(log_dt.astype(F32))[:, None]
    mag = jnp.exp(lr * dt)
    a_re = mag * jnp.cos(li * dt)
    a_im = mag * jnp.sin(li * dt)
    den = lr * lr + li * li
    coef_re = ((a_re - 1.0) * lr + a_im * li) / den
    coef_im = (a_im * lr - (a_re - 1.0) * li) / den
    br = b_re.astype(F32)
    bi = b_im.astype(F32)
    bbar_re = coef_re[..., None] * br - coef_im[..., None] * bi
    bbar_im = coef_re[..., None] * bi + coef_im[..., None] * br
    eye = jnp.eye(gh, dtype=F32)

    def bd_in(w):
        c = w.shape[2]
        return jnp.einsum('gpc,gh->gchp', w, eye).reshape(gh * c, gh * p)

    def bd_out(w):
        c = w.shape[1]
        return jnp.einsum('gcp,gh->gphc', w, eye).reshape(gh * p, gh * c)

    lt = S5_LANE_TILE
    nt = gh * p // lt

    def tile_cols(re, im):
        r = re.shape[0]
        return jnp.stack([re.reshape(r, nt, lt), im.reshape(r, nt, lt)], axis=2).reshape(r, 2 * nt * lt)

    def tile_rows(re, im):
        c = re.shape[1]
        return jnp.stack([re.reshape(nt, lt, c), im.reshape(nt, lt, c)], axis=1).reshape(2 * nt * lt, c)

    bbd = jnp.stack([tile_cols(bd_in(bbar_re[i * gh:(i + 1) * gh]), bd_in(bbar_im[i * gh:(i + 1) * gh]))
                     for i in range(2)])
    cbd = jnp.stack([tile_rows(bd_out(c_re.astype(F32)[i * gh:(i + 1) * gh]),
                               -bd_out(c_im.astype(F32)[i * gh:(i + 1) * gh])) for i in range(2)])
    r = 1 << (int(math.log2(t)) // 2)
    c = t // 2
    ln_mag = (lr * dt).reshape(1, g * p)
    ang = (li * dt).reshape(1, g * p)
    hi = jnp.arange(t // r, dtype=F32)[:, None] * float(r) - float(c)
    lo = jnp.arange(r, dtype=F32)[:, None]
    pmag = (jnp.exp(hi * ln_mag)[:, None] * jnp.exp(lo * ln_mag)[None]).reshape(t, g * p)
    nmag = (jnp.exp(-hi * ln_mag)[:, None] * jnp.exp(-lo * ln_mag)[None]).reshape(t, g * p)
    chi, shi = jnp.cos(hi * ang)[:, None], jnp.sin(hi * ang)[:, None]
    clo, slo = jnp.cos(lo * ang)[None], jnp.sin(lo * ang)[None]
    cs = (chi * clo - shi * slo).reshape(t, g * p)
    sn = (shi * clo + chi * slo).reshape(t, g * p)
    cmag = jnp.exp(float(c + 1) * ln_mag)
    return dict(bbd=bbd.astype(BF16), cbd=cbd.astype(BF16),
                npre=nmag * cs, npim=-nmag * sn, ppre=pmag * cs, ppim=pmag * sn,
                are=cmag * jnp.cos(float(c + 1) * ang), aim=cmag * jnp.sin(float(c + 1) * ang))


def _even_layer(x, g_in, g_out, w_in, w_out, lam_re, lam_im, log_dt, b_re, b_im, c_re, c_im,
                d_skip, w_glu, sgu_g, sgu_w, sgu_b):
    bsz, seq, d = x.shape
    dh = d_skip.shape[0]
    t = min(MIX_CHUNK, seq)
    tb = _s5_tables(lam_re, lam_im, log_dt, b_re, b_im, c_re, c_im, t)
    ns = tb['npre'].shape[1]
    tri = jnp.asarray(np.tril(np.ones((t, t), np.float32)), BF16)
    mask = np.tril(np.ones((SGU_CHUNK, SGU_CHUNK), dtype=bool))
    sguw = jnp.where(mask, sgu_w.astype(F32), 0.0).astype(BF16)
    hd = dh // SGU_HEADS
    sgub = jnp.repeat(sgu_b.astype(F32).T, hd, axis=1)
    nbat = min(MIX_BATCH, bsz)
    xspec = pl.BlockSpec((nbat, t, d), lambda b, c: (b, c, 0))
    args = [x, g_in.reshape(1, d), g_out.reshape(1, d), w_in, tb['bbd'], tb['cbd'],
            tb['npre'], tb['npim'], tb['ppre'], tb['ppim'], tb['are'], tb['aim'], tri,
            d_skip.reshape(1, dh).astype(F32), w_glu, sgu_g.reshape(1, dh).astype(F32),
            sguw, sgub, w_out]
    return pl.pallas_call(
        _even_kernel,
        grid=(bsz // nbat, seq // t),
        in_specs=[xspec] + [_const_spec(a.shape) for a in args[1:]],
        out_specs=xspec,
        out_shape=jax.ShapeDtypeStruct(x.shape, F32),
        scratch_shapes=[pltpu.VMEM((nbat, 1, ns), F32), pltpu.VMEM((nbat, 1, ns), F32)],
        compiler_params=pltpu.CompilerParams(
            dimension_semantics=("arbitrary", "arbitrary"), vmem_limit_bytes=VMEM_LIMIT),
        name="even_mixer",
    )(*args)


def _odd_kernel(x_ref, *refs):
    halo_ref, st_ref = refs[-2:]

    @pl.when(pl.program_id(1) == 0)
    def _():
        halo_ref[...] = jnp.zeros_like(halo_ref)
        st_ref[...] = jnp.zeros_like(st_ref)

    _skewed_phases(functools.partial(_odd_sequence, x_ref, *refs), x_ref.shape[0])


def _odd_sequence(x_ref, gin_ref, gout_ref, win_ref, pm_ref, poolw_ref, pscale_ref, lb_ref,
                  dall_ref, lvl_ref, onorm_ref, wout_ref, o_ref, halo_ref, st_ref, bi):
    t = x_ref.shape[1]
    dh = lb_ref.shape[1]
    hd = HGRN_HEAD_DIM
    nheads = dh // hd
    blk = lvl_ref.shape[0]
    nb = t // blk
    nlev = int(math.log2(blk))

    x = x_ref[bi]
    h = _rms(x, gin_ref[...]).astype(BF16)
    z = _dot(h, win_ref[...])
    yield

    zc = z[:, :dh].astype(BF16)
    zext = jnp.concatenate([halo_ref[bi].astype(BF16), zc], axis=0)
    halo_ref[bi] = zc[t - POOL_HALO:, :].astype(F32)
    gd = dh // len(POOL_WINDOWS)
    pooled = []
    later = jnp.minimum(pl.program_id(1), 1)
    for gi in range(len(POOL_WINDOWS)):
        m = _dot(pm_ref[later, gi], zext[:, gi * gd:(gi + 1) * gd])
        pooled.append(_dot(m.astype(BF16), poolw_ref[gi]))
    yc = jnp.concatenate(pooled, axis=1) * pscale_ref[...]

    lb = lb_ref[...]
    q = jax.nn.silu(z[:, dh:2 * dh])
    fg = lb + (1.0 - lb) * jax.nn.sigmoid(z[:, 2 * dh:3 * dh])
    logf = jnp.log(fg) * LOG2_E
    k = 1.0 - fg
    v = z[:, 3 * dh:4 * dh].astype(BF16)
    hi = logf.astype(BF16)
    lo = (logf - hi.astype(F32)).astype(BF16)
    dall = dall_ref[...]
    gall = _dot(dall, hi) + _dot(dall, lo)
    b = gall[t:, :]
    lvl = lvl_ref[...]

    def level_decay(lev):
        if lev == 1:
            return jnp.exp2(-jnp.abs(gall[:t, :]))
        n = 1 << lev
        parts = []
        for s0 in range(0, t, 2 * n):
            row = b[s0 + n - 1:s0 + n, :]
            if n % 8 == 0:
                m = jnp.broadcast_to(row, (n, dh))
                parts += [jnp.exp2(m - b[s0:s0 + n, :]), jnp.exp2(b[s0 + n:s0 + 2 * n, :] - m)]
            else:
                parts.append(jnp.exp2(-jnp.abs(b[s0:s0 + 2 * n, :] - jnp.broadcast_to(row, (2 * n, dh)))))
        return jnp.concatenate(parts, axis=0)

    def block_scores(qa, ka):
        return [[_dot_nt(qa[j * blk:(j + 1) * blk, i * hd:(i + 1) * hd],
                         ka[j * blk:(j + 1) * blk, i * hd:(i + 1) * hd]) for j in range(nb)]
                for i in range(nheads)]

    qb, kb = q.astype(BF16), k.astype(BF16)
    scores = [[None] * nb for _ in range(nheads)]

    def lev_decay(i, _):
        if i == 0:
            return None
        if i == 1:
            return fg.astype(BF16)
        return level_decay(i - 1).astype(BF16)

    def lev_operands(i, e):
        if i == 0:
            return qb, kb
        return qb * e, (kb if i == 1 else kb * e)

    def lev_scores(i, ops):
        return block_scores(*ops)

    def lev_select(i, w):
        sel = lvl == (nlev if i == 0 else i - 1)
        for hh in range(nheads):
            for j in range(nb):
                scores[hh][j] = jnp.where(sel, w[hh][j], 0.0 if i == 0 else scores[hh][j])

    _skewed([lev_decay, lev_operands, lev_scores, lev_select], nlev + 1)

    cross = [None] * nb
    for j in range(1, nb):
        m = b[j * blk - 1:j * blk, :]
        qx = (q[j * blk:(j + 1) * blk, :] * jnp.exp2(b[j * blk:(j + 1) * blk, :] - m)).astype(BF16)
        kx = (k[:j * blk, :] * jnp.exp2(m - b[:j * blk, :])).astype(BF16)
        cross[j] = [_dot_nt(qx[:, i * hd:(i + 1) * hd], kx[:, i * hd:(i + 1) * hd]) for i in range(nheads)]

    blast = b[t - 1:t, :]
    qd = (q * jnp.exp2(b)).astype(BF16)
    kd = (k * jnp.exp2(blast - b)).astype(BF16)
    dec = jnp.exp2(blast)
    yield

    def head_out(i, _):
        sl = slice(i * hd, (i + 1) * hd)
        st = st_ref[bi, i]
        rows = []
        for j in range(nb):
            a = scores[i][j] if j == 0 else jnp.concatenate([cross[j][i], scores[i][j]], axis=1)
            rows.append(_dot(a.astype(BF16), v[:(j + 1) * blk, sl]))
        o = (jnp.concatenate(rows, axis=0) if nb > 1 else rows[0]) + _dot_nt(qd[:, sl], st.astype(BF16))
        st_ref[bi, i] = st * dec[:, sl] + _dot_tn(v[:, sl], kd[:, sl])
        return o

    def head_norm(i, o):
        return o * lax.rsqrt(jnp.mean(o * o, axis=-1, keepdims=True) + EPS) * onorm_ref[...]

    outs = _skewed([head_out, head_norm], nheads)
    yd = jnp.concatenate(outs, axis=1) * jax.nn.silu(z[:, 4 * dh:])

    ycat = jnp.concatenate([yc.astype(BF16), yd.astype(BF16)], axis=1)
    out = _dot(ycat, wout_ref[...])
    o_ref[bi] = x + _rms(out, gout_ref[...])


def _pool_matrices(t):
    r = jnp.arange(t)[:, None]
    c = jnp.arange(t + POOL_HALO)[None, :] - POOL_HALO
    mats = []
    for first in (True, False):
        per = []
        for win in POOL_WINDOWS:
            band = (c <= r) & (c > r - win)
            if first:
                band = band & (c >= 0)
                cnt = jnp.minimum(r + 1, win).astype(F32)
            else:
                cnt = jnp.full((t, 1), float(win), F32)
            per.append(jnp.where(band, 1.0 / cnt, 0.0) - (c == r).astype(F32))
        mats.append(jnp.stack(per))
    return jnp.stack(mats).astype(BF16)


def _hgrn_level_matrices(t, blk):
    r = np.arange(t)[:, None]
    c = np.arange(t)[None, :]
    e = ((r >> 2) << 2) + 1
    upper = (c > e) & (c <= r)
    lower = (c > r) & (c <= e)
    mats = [upper.astype(np.float32) - lower.astype(np.float32), (c <= r).astype(np.float32)]
    nlev = int(math.log2(blk))
    r, c = r[:blk, :], c[:, :blk]
    x = r ^ c
    lvl = np.zeros((blk, blk), np.int32)
    for lev in range(1, nlev):
        lvl = lvl + (x >= (1 << lev)).astype(np.int32)
    lvl = np.where(r > c, lvl, np.where(r == c, nlev, -1)).astype(np.int32)
    return jnp.asarray(np.concatenate(mats, axis=0), BF16), jnp.asarray(lvl)


def _odd_layer(x, g_in, g_out, w_in, w_out, pool_w, pool_scale, lb, onorm_g):
    bsz, seq, d = x.shape
    dh = lb.shape[0]
    t = min(MIX_CHUNK, seq)
    pm = _pool_matrices(t)
    dall, lvl = _hgrn_level_matrices(t, min(HGRN_BLOCK, t))
    nheads = dh // HGRN_HEAD_DIM
    nbat = min(MIX_BATCH, bsz)
    xspec = pl.BlockSpec((nbat, t, d), lambda b, c: (b, c, 0))
    args = [x, g_in.reshape(1, d), g_out.reshape(1, d), w_in, pm, pool_w,
            pool_scale.reshape(1, dh).astype(F32), lb.reshape(1, dh).astype(F32), dall, lvl,
            onorm_g.reshape(1, HGRN_HEAD_DIM).astype(F32), w_out]
    return pl.pallas_call(
        _odd_kernel,
        grid=(bsz // nbat, seq // t),
        in_specs=[xspec] + [_const_spec(a.shape) for a in args[1:]],
        out_specs=xspec,
        out_shape=jax.ShapeDtypeStruct(x.shape, F32),
        scratch_shapes=[pltpu.VMEM((nbat, POOL_HALO, dh), F32),
                        pltpu.VMEM((nbat, nheads, HGRN_HEAD_DIM, HGRN_HEAD_DIM), F32)],
        compiler_params=pltpu.CompilerParams(
            dimension_semantics=("arbitrary", "arbitrary"), vmem_limit_bytes=VMEM_LIMIT),
        name="odd_mixer",
    )(*args)


def kernel(x, norm_g, ffn_wg, ffn_wu, ffn_wd, even_w_in, even_w_out, s5_lam_re, s5_lam_im, s5_log_dt, s5_b_re, s5_b_im, s5_c_re, s5_c_im, s5_d, s5_w_glu, sgu_norm_g, sgu_w, sgu_b, odd_w_in, odd_w_out, pool_w, pool_scale, hgrn_lb, hgrn_onorm_g):
    bsz, seq, d = x.shape
    depth = norm_g.shape[0]
    sm = jax.nn.softmax(hgrn_lb.astype(F32), axis=0)
    lb_all = jnp.cumsum(sm, axis=0) - sm[0:1]

    stacks = (ffn_wg, ffn_wu, ffn_wd)
    pool_w2 = pool_w.reshape(pool_w.shape[0], -1, pool_w.shape[-1])

    wd_rows = ffn_wd.reshape(ffn_wd.shape[:2] + ffn_wg.shape[2:])

    def ffn_side(li, half):
        return [(a, (li, half)) for a in (ffn_wg, ffn_wu, wd_rows)]

    def ffn_weights(cast):
        return cast[:2] + [cast[2].reshape(ffn_wd.shape[2:])] if cast else []

    def mixer_side(li):
        j = li // 2
        if li % 2 == 0:
            return [(even_w_in, (j,)), (even_w_out, (j,)), (s5_w_glu, (j,))]
        return [(odd_w_in, (j,)), (odd_w_out, (j,)), (pool_w2, (j,))]

    def mixer(xx, li, g, mw):
        j = li // 2
        if li % 2 == 0:
            return _even_layer(xx, g[2], g[3], mw[0], mw[1], s5_lam_re[j], s5_lam_im[j], s5_log_dt[j],
                               s5_b_re[j], s5_b_im[j], s5_c_re[j], s5_c_im[j], s5_d[j], mw[2],
                               sgu_norm_g[j], sgu_w[j], sgu_b[j])
        return _odd_layer(xx, g[2], g[3], mw[0], mw[1], mw[2].reshape(pool_w.shape[1:]), pool_scale[j],
                          lb_all[li], hgrn_onorm_g[j])

    gs = [norm_g[li].astype(F32) for li in range(depth)]
    x2 = x.reshape(bsz * seq, d)
    w_first = [a[0, 0].astype(BF16) for a in stacks]
    side = mixer_side(0) + ffn_side(0, 1) + (ffn_side(1, 0) if depth > 1 else [])
    x2, cast = _ffn(x2, [(gs[0][0], gs[0][1], w_first)], side)
    mw, w_tail, w_head = cast[:3], ffn_weights(cast[3:6]), ffn_weights(cast[6:9])
    for li in range(depth):
        x2 = mixer(x2.reshape(bsz, seq, d), li, gs[li], mw).reshape(bsz * seq, d)
        ffns = [(gs[li][4], gs[li][5], w_tail)]
        side = []
        if li + 1 < depth:
            ffns.append((gs[li + 1][0], gs[li + 1][1], w_head))
            side = mixer_side(li + 1) + ffn_side(li + 1, 1) + (ffn_side(li + 2, 0) if li + 2 < depth else [])
        x2, cast = _ffn(x2, ffns, side)
        mw, w_tail, w_head = cast[:3], ffn_weights(cast[3:6]), ffn_weights(cast[6:9])
    return x2.reshape(bsz, seq, d)
```
